```python
import jax, jax.numpy as jnp
from jax import lax
import numpy as np

D_MODEL = 2048
BATCH = 2
SEQ = 4096
DEPTH = 4
DEC_BATCH = 8
DEC_SEQ = 1
PAST_LEN = 16384
PAGE_SIZE = 128

N_MIXERS = 2
N_GMLP = (DEPTH + 1) // 2
N_FOX = DEPTH // 2
CHUNK = 128
GM_WIDTH = D_MODEL
GM_GROUPS = 16
GM_GDIM = GM_WIDTH // GM_GROUPS
FOX_HEADS = 16
FOX_HDIM = D_MODEL // FOX_HEADS
Q_BLOCK = 128
FORGET_BIAS = 9.0
N_GROUPS = 4
EXP_PER_GROUP = 4
N_EXPERTS = N_GROUPS * EXP_PER_GROUP
TOP_K = 2
D_EXPERT = 512

RMS_EPS = 1e-6
LN_EPS = 1e-5
NEG_INF = -1e30

kernel_name = "hybrid_gmlp_fox_hmoe_step"


def rmsnorm(x, g):
    xf = x.astype(jnp.float32)
    y = xf * lax.rsqrt(jnp.mean(xf * xf, axis=-1, keepdims=True) + RMS_EPS)
    return (y * g.astype(jnp.float32)).astype(x.dtype)


def layernorm(x, g, b):
    xf = x.astype(jnp.float32)
    mu = jnp.mean(xf, axis=-1, keepdims=True)
    xc = xf - mu
    y = xc * lax.rsqrt(jnp.mean(xc * xc, axis=-1, keepdims=True) + LN_EPS)
    return (y * g.astype(jnp.float32) + b.astype(jnp.float32)).astype(x.dtype)


def ada_mod(c, w, b):
    m = jax.nn.silu(c) @ w + b
    return m.reshape(c.shape[0], 6, D_MODEL)


def modulate(h, shift, scale):
    return h * (1.0 + scale[:, None, :]) + shift[:, None, :]


def chunk_spatial_mix(v, w_s, b_s):
    bsz, L, _ = v.shape
    n_chunks = -(-L // CHUNK)
    pad = n_chunks * CHUNK - L
    vp = jnp.pad(v, ((0, 0), (0, pad), (0, 0))) if pad else v
    vc = vp.reshape(bsz, n_chunks, CHUNK, GM_GROUPS, GM_GDIM)
    mask = jnp.tril(jnp.ones((CHUNK, CHUNK), dtype=bool))
    ws = jnp.where(mask, w_s, 0)
    s = jnp.einsum('gts,bcsgd->bctgd', ws, vc) + b_s.T[None, None, :, :, None]
    return s.reshape(bsz, n_chunks * CHUNK, GM_WIDTH)[:, :L]


def chunk_gmlp(h, w_in, b_in, ln_g, ln_b, w_s, b_s, w_out, b_out):
    z = jax.nn.gelu(h @ w_in + b_in)
    u, v = jnp.split(z, 2, axis=-1)
    v = layernorm(v, ln_g, ln_b)
    s = chunk_spatial_mix(v, w_s, b_s)
    return (u * s) @ w_out + b_out, v


def fox_project(h, w_qkv, w_f, b_f):
    bsz, L, _ = h.shape
    qkv = (h @ w_qkv).reshape(bsz, L, 3, FOX_HEADS, FOX_HDIM)
    q, k, v = qkv[:, :, 0], qkv[:, :, 1], qkv[:, :, 2]
    logf = jax.nn.log_sigmoid((h @ w_f + b_f).astype(jnp.float32))
    return q, k, v, logf


def fox_prompt_attn(q, k, v, logf):
    bsz, L = q.shape[:2]
    nb = L // Q_BLOCK
    scale = FOX_HDIM ** -0.5
    cum = jnp.cumsum(logf, axis=1)
    cum_k = cum.transpose(0, 2, 1)
    kpos = jnp.arange(L)
    qb = q.reshape(bsz, nb, Q_BLOCK, FOX_HEADS, FOX_HDIM).transpose(1, 0, 2, 3, 4)
    cb = cum.reshape(bsz, nb, Q_BLOCK, FOX_HEADS).transpose(1, 0, 3, 2)

    def block(args):
        i, q_i, c_i = args
        s = jnp.einsum('bqhd,bkhd->bhqk', q_i, k).astype(jnp.float32) * scale
        s = s + c_i[..., None] - cum_k[:, :, None, :]
        qpos = i * Q_BLOCK + jnp.arange(Q_BLOCK)
        s = jnp.where(kpos[None, :] <= qpos[:, None], s, NEG_INF)
        p = jax.nn.softmax(s, axis=-1)
        return jnp.einsum('bhqk,bkhd->bqhd', p.astype(v.dtype), v)

    out = lax.map(block, (jnp.arange(nb), qb, cb))
    return out.transpose(1, 0, 2, 3, 4).reshape(bsz, L, FOX_HEADS * FOX_HDIM)


def fox_sample_attn(q, k_new, v_new, lf_new, cache_k, cache_v, cache_logf, page_table, layer):
    bsz, Ln = q.shape[:2]
    n_pages = page_table.shape[1]
    scale = FOX_HDIM ** -0.5
    lf_past = cache_logf[layer, page_table].astype(jnp.float32)
    lf_flat = lf_past.reshape(bsz, n_pages * PAGE_SIZE, FOX_HEADS)
    rest = lax.cumsum(lf_flat, axis=1, reverse=True) - lf_flat
    rest = rest.reshape(bsz, n_pages, PAGE_SIZE, FOX_HEADS).transpose(1, 0, 3, 2)
    c_new = jnp.cumsum(lf_new, axis=1).transpose(0, 2, 1)

    def step(carry, xs):
        m, l, acc = carry
        pt_col, rest_p = xs
        kp = cache_k[layer, pt_col]
        vp = cache_v[layer, pt_col]
        s = jnp.einsum('bqhd,bkhd->bhqk', q, kp).astype(jnp.float32) * scale
        s = s + c_new[..., None] + rest_p[:, :, None, :]
        m_new = jnp.maximum(m, s.max(axis=-1))
        alpha = jnp.exp(m - m_new)
        p = jnp.exp(s - m_new[..., None])
        l = l * alpha + p.sum(axis=-1)
        acc = acc * alpha[..., None] + jnp.einsum('bhqk,bkhd->bhqd', p, vp.astype(jnp.float32))
        return (m_new, l, acc), None

    init = (jnp.full((bsz, FOX_HEADS, Ln), NEG_INF, jnp.float32),
            jnp.zeros((bsz, FOX_HEADS, Ln), jnp.float32),
            jnp.zeros((bsz, FOX_HEADS, Ln, FOX_HDIM), jnp.float32))
    (m, l, acc), _ = lax.scan(step, init, (page_table.T, rest))

    s = jnp.einsum('bqhd,bkhd->bhqk', q, k_new).astype(jnp.float32) * scale
    s = s + c_new[..., :, None] - c_new[..., None, :]
    causal = jnp.tril(jnp.ones((Ln, Ln), dtype=bool))
    s = jnp.where(causal, s, NEG_INF)
    m_new = jnp.maximum(m, s.max(axis=-1))
    alpha = jnp.exp(m - m_new)
    p = jnp.exp(s - m_new[..., None])
    l = l * alpha + p.sum(axis=-1)
    acc = acc * alpha[..., None] + jnp.einsum('bhqk,bkhd->bhqd', p, v_new.astype(jnp.float32))
    out = acc / l[..., None]
    return out.transpose(0, 2, 1, 3).reshape(bsz, Ln, FOX_HEADS * FOX_HDIM).astype(q.dtype)


def hier_moe(h, w_grp, b_grp, w_exp, b_exp, w1, w3, w2):
    T = h.shape[0]
    g_logits = (h @ w_grp + b_grp).astype(jnp.float32)
    g_prob = jax.nn.softmax(g_logits, axis=-1)
    g_idx = jnp.argmax(g_logits, axis=-1)
    g_gate = jnp.take_along_axis(g_prob, g_idx[:, None], axis=-1)
    e_logits = (h @ w_exp + b_exp).astype(jnp.float32).reshape(T, N_GROUPS, EXP_PER_GROUP)
    sel = jnp.broadcast_to(g_idx[:, None, None], (T, 1, EXP_PER_GROUP))
    e_sel = jnp.take_along_axis(e_logits, sel, axis=1)[:, 0]
    top_v, top_i = lax.top_k(e_sel, TOP_K)
    top_w = jax.nn.softmax(top_v, axis=-1) * g_gate
    expert_id = g_idx[:, None] * EXP_PER_GROUP + top_i
    combine = jnp.sum(jax.nn.one_hot(expert_id, N_EXPERTS, dtype=jnp.float32) * top_w[..., None], axis=1)
    a = jnp.einsum('td,edf->tef', h, w1)
    b = jnp.einsum('td,edf->tef', h, w3)
    hid = jax.nn.silu(a) * b * combine[..., None].astype(h.dtype)
    return jnp.einsum('tef,efd->td', hid, w2)


def setup_inputs(seed: int = 0) -> dict:
    key = jax.random.key(seed)
    ks = iter(jax.random.split(key, 40))
    f32 = jnp.float32
    n_pages = PAST_LEN // PAGE_SIZE
    n_used = DEC_BATCH * n_pages
    n_pool = n_used + max(1, n_used // 4)

    def nrm(shape, std):
        return jax.random.normal(next(ks), shape, f32) * std

    sd = D_MODEL ** -0.5
    inp = {}
    inp["x_prompt"] = nrm((BATCH, SEQ, D_MODEL), 1.0)
    inp["x_sample"] = nrm((DEC_BATCH, DEC_SEQ, D_MODEL), 1.0)
    inp["cache_k"] = nrm((N_FOX, n_pool, PAGE_SIZE, FOX_HEADS, FOX_HDIM), 1.0)
    inp["cache_v"] = nrm((N_FOX, n_pool, PAGE_SIZE, FOX_HEADS, FOX_HDIM), 1.0)
    inp["cache_logf"] = jax.nn.log_sigmoid(FORGET_BIAS + nrm((N_FOX, n_pool, PAGE_SIZE, FOX_HEADS), 0.5))
    perm = jax.random.permutation(next(ks), n_pool)
    inp["page_table"] = perm[:n_used].reshape(DEC_BATCH, n_pages).astype(jnp.int32)
    inp["c_prompt"] = nrm((BATCH, D_MODEL), 1.0)
    inp["c_sample"] = nrm((DEC_BATCH, D_MODEL), 1.0)
    inp["w_ada"] = nrm((DEPTH, D_MODEL, 6 * D_MODEL), 0.5 * sd)
    inp["b_ada"] = nrm((DEPTH, 6 * D_MODEL), 0.02)
    inp["norm1_g"] = 1.0 + nrm((DEPTH, D_MODEL), 0.02)
    inp["norm2_g"] = 1.0 + nrm((DEPTH, D_MODEL), 0.02)
    inp["final_g"] = 1.0 + nrm((D_MODEL,), 0.02)
    inp["gm_w_in"] = nrm((N_GMLP, D_MODEL, 2 * GM_WIDTH), sd)
    inp["gm_b_in"] = nrm((N_GMLP, 2 * GM_WIDTH), 0.02)
    inp["gm_ln_g"] = 1.0 + nrm((N_GMLP, GM_WIDTH), 0.02)
    inp["gm_ln_b"] = nrm((N_GMLP, GM_WIDTH), 0.02)
    inp["gm_ws"] = nrm((N_GMLP, GM_GROUPS, CHUNK, CHUNK), CHUNK ** -0.5)
    inp["gm_bs"] = 1.0 + nrm((N_GMLP, GM_GROUPS, CHUNK), 0.1)
    inp["gm_w_out"] = nrm((N_GMLP, GM_WIDTH, D_MODEL), GM_WIDTH ** -0.5)
    inp["gm_b_out"] = nrm((N_GMLP, D_MODEL), 0.02)
    inp["fox_w_qkv"] = nrm((N_FOX, D_MODEL, 3 * D_MODEL), sd)
    inp["fox_w_f"] = nrm((N_FOX, D_MODEL, FOX_HEADS), 0.3 * sd)
    inp["fox_b_f"] = FORGET_BIAS + nrm((N_FOX, FOX_HEADS), 0.5)
    inp["fox_w_o"] = nrm((N_FOX, D_MODEL, D_MODEL), sd)
    inp["moe_w_grp"] = nrm((DEPTH, D_MODEL, N_GROUPS), sd)
    inp["moe_b_grp"] = nrm((DEPTH, N_GROUPS), 0.01)
    inp["moe_w_exp"] = nrm((DEPTH, D_MODEL, N_EXPERTS), sd)
    inp["moe_b_exp"] = nrm((DEPTH, N_EXPERTS), 0.01)
    inp["moe_w1"] = nrm((DEPTH, N_EXPERTS, D_MODEL, D_EXPERT), sd)
    inp["moe_w3"] = nrm((DEPTH, N_EXPERTS, D_MODEL, D_EXPERT), sd)
    inp["moe_w2"] = nrm((DEPTH, N_EXPERTS, D_EXPERT, D_MODEL), D_EXPERT ** -0.5)
    return inp


def reference(x_prompt, x_sample, cache_k, cache_v, cache_logf, page_table, c_prompt, c_sample,
              w_ada, b_ada, norm1_g, norm2_g, final_g,
              gm_w_in, gm_b_in, gm_ln_g, gm_ln_b, gm_ws, gm_bs, gm_w_out, gm_b_out,
              fox_w_qkv, fox_w_f, fox_b_f, fox_w_o,
              moe_w_grp, moe_b_grp, moe_w_exp, moe_b_exp, moe_w1, moe_w3, moe_w2):
    xp, xs = x_prompt, x_sample
    k_p, v_p, lf_p, k_s, v_s, lf_s, gv_s = [], [], [], [], [], [], []
    for i in range(DEPTH):
        slot = i // N_MIXERS
        mp = ada_mod(c_prompt, w_ada[i], b_ada[i])
        ms = ada_mod(c_sample, w_ada[i], b_ada[i])
        hp = modulate(rmsnorm(xp, norm1_g[i]), mp[:, 0], mp[:, 1])
        hs = modulate(rmsnorm(xs, norm1_g[i]), ms[:, 0], ms[:, 1])
        if i % N_MIXERS == 0:
            gw = (gm_w_in[slot], gm_b_in[slot], gm_ln_g[slot], gm_ln_b[slot],
                  gm_ws[slot], gm_bs[slot], gm_w_out[slot], gm_b_out[slot])
            op, _ = chunk_gmlp(hp, *gw)
            o_s, gv = chunk_gmlp(hs, *gw)
            gv_s.append(gv)
        else:
            qp, kp, vp, lfp = fox_project(hp, fox_w_qkv[slot], fox_w_f[slot], fox_b_f[slot])
            qs, kn, vn, lfn = fox_project(hs, fox_w_qkv[slot], fox_w_f[slot], fox_b_f[slot])
            op = fox_prompt_attn(qp, kp, vp, lfp) @ fox_w_o[slot]
            o_s = fox_sample_attn(qs, kn, vn, lfn, cache_k, cache_v, cache_logf, page_table, slot) @ fox_w_o[slot]
            k_p.append(kp); v_p.append(vp); lf_p.append(lfp)
            k_s.append(kn); v_s.append(vn); lf_s.append(lfn)
        xp = xp + mp[:, 2][:, None, :] * op
        xs = xs + ms[:, 2][:, None, :] * o_s
        mw = (moe_w_grp[i], moe_b_grp[i], moe_w_exp[i], moe_b_exp[i], moe_w1[i], moe_w3[i], moe_w2[i])
        hp = modulate(rmsnorm(xp, norm2_g[i]), mp[:, 3], mp[:, 4])
        hs = modulate(rmsnorm(xs, norm2_g[i]), ms[:, 3], ms[:, 4])
        xp = xp + mp[:, 5][:, None, :] * hier_moe(hp.reshape(-1, D_MODEL), *mw).reshape(xp.shape)
        xs = xs + ms[:, 5][:, None, :] * hier_moe(hs.reshape(-1, D_MODEL), *mw).reshape(xs.shape)
    y_prompt = rmsnorm(xp, final_g)
    y_sample = rmsnorm(xs, final_g)
    return (y_prompt, y_sample, jnp.stack(k_p), jnp.stack(v_p), jnp.stack(lf_p),
            jnp.stack(k_s), jnp.stack(v_s), jnp.stack(lf_s), jnp.stack(gv_s))
```

```python
import functools
import math

import jax
import jax.numpy as jnp
from jax import lax
from jax.experimental import pallas as pl
from jax.experimental.pallas import tpu as pltpu

F32 = jnp.float32
BF16 = jnp.bfloat16

RMS_EPS = 1e-6
LN_EPS = 1e-5
NEG_INF = -1e30
CHUNK = 128
GM_GROUPS = 16
N_GROUPS = 4
EXP_PER_GROUP = 4
N_EXPERTS = N_GROUPS * EXP_PER_GROUP
LANES = 128
BF16_ROWS = 16
ADA_ROWS = 16

V7X_VMEM_LIMIT = 56 * 1024 * 1024


def _cparams(sem, vmem=V7X_VMEM_LIMIT):
    return pltpu.CompilerParams(dimension_semantics=sem, vmem_limit_bytes=vmem)


def _silu(x):
    return x / (1.0 + jnp.exp(-x))


def _gelu_tanh(x):
    return 0.5 * x * (1.0 + jnp.tanh(math.sqrt(2.0 / math.pi) * (x + 0.044715 * (x * x * x))))


def _log_sigmoid(x):
    y = -x
    return -(jnp.maximum(y, 0.0) + jnp.log1p(jnp.exp(-jnp.abs(y))))


def _rmsnorm(x, g):
    return x * lax.rsqrt(jnp.mean(x * x, axis=-1, keepdims=True) + RMS_EPS) * g


def _stack_split(a, terms):
    parts = []
    r = a
    for _ in range(terms):
        p = r.astype(BF16).astype(F32)
        parts.append(p)
        r = r - p
    return jnp.concatenate(parts, axis=0).astype(BF16)


def _fold_rows(r, m, terms):
    out = r[:m]
    for t in range(1, terms):
        out = out + r[t * m:(t + 1) * m]
    return out


def _dot_lp(a, b):
    m = a.shape[0]
    pad = (-m) % BF16_ROWS
    if pad:
        a = jnp.concatenate([a, jnp.zeros((pad, a.shape[1]), F32)], axis=0)
    r = jnp.dot(a.astype(BF16), b.astype(BF16), preferred_element_type=F32)
    return r[:m] if pad else r


def _dot_split_exact_rhs(a, b_bf16, terms=2):
    m = a.shape[0]
    mp = m + (-m) % BF16_ROWS
    if mp != m:
        a = jnp.concatenate([a, jnp.zeros((mp - m, a.shape[1]), F32)], axis=0)
    r = jnp.dot(_stack_split(a, terms), b_bf16, preferred_element_type=F32)
    return _fold_rows(r, mp, terms)[:m]


def _dot_exact_lhs_split(a_bf16, b, terms=3):
    out = None
    r = b
    for _ in range(terms):
        p = r.astype(BF16)
        d = jnp.dot(a_bf16, p, preferred_element_type=F32)
        out = d if out is None else out + d
        r = r - p.astype(F32)
    return out


def _ada_kernel(c_ref, w_ref, b_ref, o_ref):
    o_ref[...] = _dot_lp(_silu(c_ref[...]), w_ref[...]) + b_ref[...]


def _ada_all(c16, w_ada, b_ada, tn=1024):
    depth, d, n6 = w_ada.shape
    return pl.pallas_call(
        _ada_kernel,
        grid=(depth, n6 // tn),
        in_specs=[pl.BlockSpec((ADA_ROWS, d), lambda l, j: (0, 0)),
                  pl.BlockSpec((None, d, tn), lambda l, j: (l, 0, j)),
                  pl.BlockSpec((None, 1, tn), lambda l, j: (l, 0, j))],
        out_specs=pl.BlockSpec((None, ADA_ROWS, tn), lambda l, j: (l, 0, j)),
        out_shape=jax.ShapeDtypeStruct((depth, ADA_ROWS, n6), F32),
        compiler_params=_cparams(("arbitrary", "arbitrary")),
        name="ada_all",
    )(c16, w_ada, b_ada.reshape(depth, 1, n6))


def _norm_mod_kernel(x_ref, g_ref, sh_ref, sc_ref, o_ref):
    y = _rmsnorm(x_ref[...], g_ref[...])
    o_ref[...] = (y * (1.0 + sc_ref[...]) + sh_ref[...]).astype(o_ref.dtype)


def _norm_kernel(x_ref, g_ref, o_ref):
    o_ref[...] = _rmsnorm(x_ref[...], g_ref[...]).astype(o_ref.dtype)


def _mod_spec(layer, slot, tiles_per_batch, d):
    return pl.BlockSpec((None, None, 1, d), lambda i, *_: (layer, i // tiles_per_batch, 0, slot))


def _norm_mod_prompt(x, g, m4, layer, slot_shift, seq, tm=512):
    t, d = x.shape
    tpb = seq // tm
    return pl.pallas_call(
        _norm_mod_kernel,
        grid=(t // tm,),
        in_specs=[pl.BlockSpec((tm, d), lambda i: (i, 0)),
                  pl.BlockSpec((1, d), lambda i: (0, 0)),
                  _mod_spec(layer, slot_shift, tpb, d),
                  _mod_spec(layer, slot_shift + 1, tpb, d)],
        out_specs=pl.BlockSpec((tm, d), lambda i: (i, 0)),
        out_shape=jax.ShapeDtypeStruct((t, d), BF16),
        compiler_params=_cparams(("arbitrary",)),
        name="norm_mod_prompt",
    )(x, g.reshape(1, d), m4, m4)


def _final_norm(x, g, tm):
    t, d = x.shape
    return pl.pallas_call(
        _norm_kernel,
        grid=(t // tm,),
        in_specs=[pl.BlockSpec((tm, d), lambda i: (i, 0)), pl.BlockSpec((1, d), lambda i: (0, 0))],
        out_specs=pl.BlockSpec((tm, d), lambda i: (i, 0)),
        out_shape=jax.ShapeDtypeStruct((t, d), F32),
        compiler_params=_cparams(("arbitrary",)),
        name="final_norm",
    )(x, g.reshape(1, d))


def _mm_kernel(*refs, kind, out_scale):
    if kind == "gelu":
        a_ref, w_ref, b_ref, z_ref, wb_ref = refs
    elif kind == "resid":
        a_ref, w_ref, b_ref, x_ref, gate_ref, xo_ref, wb_ref = refs
    elif kind == "proj_bf16":
        a_ref, w_ref, ob_ref, wb_ref = refs
    else:
        a_ref, w_ref, of_ref, ob_ref, wb_ref = refs

    @pl.when(pl.program_id(1) == 0)
    def _():
        wb_ref[...] = w_ref[...].astype(BF16)

    acc = jnp.dot(a_ref[...], wb_ref[...], preferred_element_type=F32)
    if kind == "gelu":
        z_ref[...] = _gelu_tanh(acc + b_ref[...]).astype(z_ref.dtype)
    elif kind == "resid":
        xo_ref[...] = x_ref[...] + gate_ref[...] * (acc + b_ref[...])
    elif kind == "proj_bf16":
        ob_ref[...] = (acc * out_scale).astype(BF16)
    else:
        of_ref[...] = acc
        ob_ref[...] = acc.astype(BF16)


def _mm_prompt(kind, a, w, *, col0=0, n=None, bias=None, x=None, m4=None, layer=None, gate_slot=None, seq=None,
               out_scale=1.0,
               tm=512, tn=1024):
    t, k = a.shape
    n = w.shape[1] if n is None else n
    jb = col0 // tn
    a_spec = pl.BlockSpec((tm, k), lambda j, i: (i, 0))
    w_spec = pl.BlockSpec((k, tn), lambda j, i: (0, j + jb))
    o_spec = pl.BlockSpec((tm, tn), lambda j, i: (i, j))
    ins, in_specs = [a, w], [a_spec, w_spec]
    if kind in ("gelu", "resid"):
        ins.append(bias.reshape(1, n))
        in_specs.append(pl.BlockSpec((1, tn), lambda j, i: (0, j)))
    if kind == "resid":
        tpb = seq // tm
        ins += [x, m4]
        in_specs += [o_spec, pl.BlockSpec((None, None, 1, tn),
                                          lambda j, i: (layer, i // tpb, 0, gate_slot * (n // tn) + j))]
    if kind == "gelu":
        out_shape, out_specs = jax.ShapeDtypeStruct((t, n), BF16), o_spec
    elif kind == "resid":
        out_shape, out_specs = jax.ShapeDtypeStruct((t, n), F32), o_spec
    elif kind == "proj_bf16":
        out_shape, out_specs = jax.ShapeDtypeStruct((t, n), BF16), o_spec
    else:
        out_shape = (jax.ShapeDtypeStruct((t, n), F32), jax.ShapeDtypeStruct((t, n), BF16))
        out_specs = (o_spec, o_spec)
    return pl.pallas_call(
        functools.partial(_mm_kernel, kind=kind, out_scale=out_scale),
        grid=(n // tn, t // tm),
        in_specs=in_specs, out_specs=out_specs, out_shape=out_shape,
        scratch_shapes=[pltpu.VMEM((k, tn), BF16)],
        compiler_params=_cparams(("arbitrary", "arbitrary")),
        name="mm_" + kind,
    )(*ins)


def _sgu_kernel(z_ref, lng_ref, lnb_ref, ws_ref, bst_ref, o_ref, wt_ref, *, width):
    @pl.when(pl.program_id(0) == 0)
    def _():
        r = lax.broadcasted_iota(jnp.int32, (CHUNK, CHUNK), 0)
        c = lax.broadcasted_iota(jnp.int32, (CHUNK, CHUNK), 1)
        for g in range(GM_GROUPS):
            wt_ref[g] = jnp.where(c <= r, ws_ref[g], 0.0).astype(BF16)

    v = z_ref[:, width:].astype(F32)
    mu = jnp.mean(v, axis=-1, keepdims=True)
    vc = v - mu
    vn = vc * lax.rsqrt(jnp.mean(vc * vc, axis=-1, keepdims=True) + LN_EPS) * lng_ref[...] + lnb_ref[...]
    vb = vn.astype(BF16)
    gd = width // GM_GROUPS
    for g in range(GM_GROUPS):
        lo, hi = g * gd, (g + 1) * gd
        s = jnp.dot(wt_ref[g], vb[:, lo:hi], preferred_element_type=F32) + bst_ref[:, g:g + 1]
        o_ref[:, lo:hi] = (z_ref[:, lo:hi].astype(F32) * s).astype(BF16)


def _sgu_prompt(z, ln_g, ln_b, ws, bs):
    t, n2 = z.shape
    width = n2 // 2
    return pl.pallas_call(
        functools.partial(_sgu_kernel, width=width),
        grid=(t // CHUNK,),
        in_specs=[pl.BlockSpec((CHUNK, n2), lambda i: (i, 0)),
                  pl.BlockSpec((1, width), lambda i: (0, 0)),
                  pl.BlockSpec((1, width), lambda i: (0, 0)),
                  pl.BlockSpec((GM_GROUPS, CHUNK, CHUNK), lambda i: (0, 0, 0)),
                  pl.BlockSpec((CHUNK, GM_GROUPS), lambda i: (0, 0))],
        out_specs=pl.BlockSpec((CHUNK, width), lambda i: (i, 0)),
        out_shape=jax.ShapeDtypeStruct((t, width), BF16),
        scratch_shapes=[pltpu.VMEM((GM_GROUPS, CHUNK, CHUNK), BF16)],
        compiler_params=_cparams(("arbitrary",)),
        name="sgu_prompt",
    )(z, ln_g.reshape(1, width), ln_b.reshape(1, width), ws, bs.T)


def _logf_kernel(h_ref, wf_ref, bf_ref, lf_ref, qa_ref, ka_ref, carry_ref, *, tm, hd):
    @pl.when(pl.program_id(1) == 0)
    def _():
        carry_ref[...] = jnp.zeros_like(carry_ref)

    logit = jnp.dot(h_ref[...], wf_ref[...].astype(BF16), preferred_element_type=F32) + bf_ref[...]
    lf = _log_sigmoid(logit)
    lf_ref[...] = lf
    r = lax.broadcasted_iota(jnp.int32, (tm, tm), 0)
    c = lax.broadcasted_iota(jnp.int32, (tm, tm), 1)
    tri = jnp.where(c <= r, 1.0, 0.0).astype(BF16)
    cum = _dot_exact_lhs_split(tri, lf, 3) + carry_ref[...]
    carry_ref[...] = cum[tm - 1:tm, :]
    lane = lax.broadcasted_iota(jnp.int32, (tm, hd), 1)
    rnd = lambda x: x.astype(BF16).astype(F32)
    for h in range(lf.shape[1]):
        c_h = cum[:, h:h + 1]
        hi = rnd(c_h)
        mid = rnd(c_h - hi)
        lo = (c_h - hi) - mid
        qa = jnp.where(lane == 0, hi, jnp.where(lane == 1, mid, jnp.where(lane == 2, lo, jnp.where(lane < 6, 1.0, 0.0))))
        ka = jnp.where(lane < 3, 1.0, jnp.where(lane == 3, -hi, jnp.where(lane == 4, -mid, jnp.where(lane == 5, -lo, 0.0))))
        qa_ref[:, h * hd:(h + 1) * hd] = qa.astype(BF16)
        ka_ref[:, h * hd:(h + 1) * hd] = ka.astype(BF16)


def _logf_prompt(h, w_f, b_f, bsz, seq, hd, tm=512):
    t, d = h.shape
    nh = w_f.shape[1]
    tpb = seq // tm
    row = lambda n: pl.BlockSpec((tm, n), lambda b, i: (b * tpb + i, 0))
    return pl.pallas_call(
        functools.partial(_logf_kernel, tm=tm, hd=hd),
        grid=(bsz, tpb),
        in_specs=[row(d), pl.BlockSpec((d, nh), lambda b, i: (0, 0)), pl.BlockSpec((1, nh), lambda b, i: (0, 0))],
        out_specs=(row(nh), row(nh * hd), row(nh * hd)),
        out_shape=(jax.ShapeDtypeStruct((t, nh), F32), jax.ShapeDtypeStruct((t, nh * hd), BF16),
                   jax.ShapeDtypeStruct((t, nh * hd), BF16)),
        scratch_shapes=[pltpu.VMEM((1, nh), F32)],
        compiler_params=_cparams(("arbitrary", "arbitrary")),
        name="logf_prompt",
    )(h, w_f, b_f.reshape(1, nh))


def _fox_attn_kernel(q_ref, qa_ref, k_ref, ka_ref, v_ref, o_ref, *, tq, hd):
    qi = pl.program_id(2)
    hps = q_ref.shape[1] // hd
    qts = [jnp.concatenate([q_ref[:, g * hd:(g + 1) * hd], qa_ref[:, g * hd:(g + 1) * hd]], axis=1) for g in range(hps)]
    ones_col = jnp.where(lax.broadcasted_iota(jnp.int32, (tq, hd), 1) == 0, 1.0, 0.0).astype(BF16)

    def block(j, carry, diagonal):
        off = pl.multiple_of(j * tq, tq)
        out = []
        for g in range(hps):
            m, acc = carry[g]
            cols = slice(g * hd, (g + 1) * hd)
            kt = jnp.concatenate([k_ref[pl.ds(off, tq), cols], ka_ref[pl.ds(off, tq), cols]], axis=1)
            vt = jnp.concatenate([v_ref[pl.ds(off, tq), cols], ones_col], axis=1)
            s = lax.dot_general(qts[g], kt, (((1,), (1,)), ((), ())), preferred_element_type=F32)
            if diagonal:
                r = lax.broadcasted_iota(jnp.int32, (tq, tq), 0)
                c = lax.broadcasted_iota(jnp.int32, (tq, tq), 1)
                s = jnp.where(c <= r, s, NEG_INF)
            m_new = jnp.maximum(m, jnp.max(s, axis=1, keepdims=True))
            alpha = jnp.exp(m - m_new)
            p = jnp.exp(s - m_new)
            acc = acc * alpha + jnp.dot(p.astype(BF16), vt, preferred_element_type=F32)
            out.append((m_new, acc))
        return tuple(out)

    init = tuple((jnp.full((tq, 1), NEG_INF, F32), jnp.zeros((tq, 2 * hd), F32)) for _ in range(hps))
    carry = lax.fori_loop(0, qi, lambda j, c: block(j, c, False), init)
    carry = block(qi, carry, True)
    for g in range(hps):
        acc = carry[g][1]
        o_ref[:, g * hd:(g + 1) * hd] = (acc[:, :hd] / acc[:, hd:hd + 1]).astype(o_ref.dtype)


def _fox_attn_prompt(q, qa, k, ka, v, bsz, seq, nh, hd, tq=512, hps=4):
    qblk = pl.BlockSpec((None, tq, hps * hd), lambda b, h, i: (b, i, h))
    kblk = pl.BlockSpec((None, seq, hps * hd), lambda b, h, i: (b, 0, h))
    return pl.pallas_call(
        functools.partial(_fox_attn_kernel, tq=tq, hd=hd),
        grid=(bsz, nh // hps, seq // tq),
        in_specs=[qblk, qblk, kblk, kblk, kblk],
        out_specs=qblk,
        out_shape=jax.ShapeDtypeStruct((bsz, seq, nh * hd), BF16),
        compiler_params=_cparams(("arbitrary", "arbitrary", "arbitrary")),
        name="fox_attn_prompt",
    )(q, qa, k, ka, v)


def _route(g_logits, e_logits):
    m = g_logits.shape[0]
    gi = lax.broadcasted_iota(jnp.int32, (m, N_GROUPS), 1)
    ei = lax.broadcasted_iota(jnp.int32, (m, N_EXPERTS), 1)
    gmax = jnp.max(g_logits, axis=1, keepdims=True)
    g_idx = jnp.min(jnp.where(g_logits == gmax, gi, N_GROUPS), axis=1, keepdims=True)
    g_gate = 1.0 / jnp.sum(jnp.exp(g_logits - gmax), axis=1, keepdims=True)
    in_group = (ei // EXP_PER_GROUP) == g_idx
    e1 = jnp.where(in_group, e_logits, -jnp.inf)
    top1 = jnp.max(e1, axis=1, keepdims=True)
    i1 = jnp.min(jnp.where(e1 == top1, ei, N_EXPERTS), axis=1, keepdims=True)
    e2 = jnp.where(ei == i1, -jnp.inf, e1)
    top2 = jnp.max(e2, axis=1, keepdims=True)
    i2 = jnp.min(jnp.where(e2 == top2, ei, N_EXPERTS), axis=1, keepdims=True)
    r = jnp.exp(top2 - top1)
    w1 = g_gate / (1.0 + r)
    w2 = g_gate * r / (1.0 + r)
    comb = jnp.where(ei == i1, w1, 0.0) + jnp.where(ei == i2, w2, 0.0)
    two = lax.broadcasted_iota(jnp.int32, (m, 2), 1)
    ids = jnp.where(two == 0, i1, i2)
    wts = jnp.where(two == 0, w1, w2)
    return comb, ids, wts


def _router_kernel(x_ref, g_ref, sh_ref, sc_ref, wg_ref, bg_ref, we_ref, be_ref, h_ref, comb_ref, ids_ref, wts_ref):
    y = _rmsnorm(x_ref[...], g_ref[...])
    h = y * (1.0 + sc_ref[...]) + sh_ref[...]
    h_ref[...] = h.astype(h_ref.dtype)
    g_logits = _dot_lp(h, wg_ref[...]) + bg_ref[...]
    e_logits = _dot_lp(h, we_ref[...]) + be_ref[...]
    comb, ids, wts = _route(g_logits, e_logits)
    comb_ref[...] = comb
    ids_ref[...] = ids
    wts_ref[...] = wts


def _router(x, g, sh_spec, sc_spec, mod_arrays, w_grp, b_grp, w_exp, b_exp, tm, h_dtype):
    t, d = x.shape
    full = lambda shape: pl.BlockSpec(shape, lambda i: (0,) * len(shape))
    row = lambda n: pl.BlockSpec((tm, n), lambda i: (i, 0))
    return pl.pallas_call(
        _router_kernel,
        grid=(t // tm,),
        in_specs=[row(d), full((1, d)), sh_spec, sc_spec,
                  full((d, N_GROUPS)), full((1, N_GROUPS)), full((d, N_EXPERTS)), full((1, N_EXPERTS))],
        out_specs=(row(d), row(N_EXPERTS), row(2), row(2)),
        out_shape=(jax.ShapeDtypeStruct((t, d), h_dtype), jax.ShapeDtypeStruct((t, N_EXPERTS), F32),
                   jax.ShapeDtypeStruct((t, 2), jnp.int32), jax.ShapeDtypeStruct((t, 2), F32)),
        compiler_params=_cparams(("arbitrary",)),
        name="moe_router",
    )(x, g.reshape(1, d), *mod_arrays, w_grp, b_grp.reshape(1, N_GROUPS), w_exp, b_exp.reshape(1, N_EXPERTS))


MOE_TB = 512
MOE_CH = BF16_ROWS
MOE_CPT = 16
MOE_NCH = (2 * MOE_TB + N_EXPERTS * (MOE_CH - 1)) // MOE_CH + 1
MOE_NR = MOE_NCH * MOE_CH
MOE_ROUTE_ROWS = 24


def _route_sort_kernel(x_ref, g_ref, sh_ref, sc_ref, wt_ref, bt_ref, srt_ref, pos_ref, nch_ref):
    tb, d = x_ref.shape
    nr = srt_ref.shape[0]
    h = _rmsnorm(x_ref[...], g_ref[...]) * (1.0 + sc_ref[...]) + sh_ref[...]
    hb = h.astype(BF16)
    lg = lax.dot_general(wt_ref[...].astype(BF16), hb, (((1,), (1,)), ((), ())), preferred_element_type=F32) + bt_ref[...]
    grow = lax.broadcasted_iota(jnp.int32, (8, tb), 0)
    gl = jnp.where(grow < N_GROUPS, lg[0:8], -jnp.inf)
    el = lg[8:8 + N_EXPERTS]
    erow = lax.broadcasted_iota(jnp.int32, (N_EXPERTS, tb), 0)
    gmax = jnp.max(gl, axis=0, keepdims=True)
    g_idx = jnp.min(jnp.where(gl == gmax, grow, 8), axis=0, keepdims=True)
    g_gate = 1.0 / jnp.sum(jnp.exp(gl - gmax), axis=0, keepdims=True)
    e1 = jnp.where(erow // EXP_PER_GROUP == g_idx, el, -jnp.inf)
    top1 = jnp.max(e1, axis=0, keepdims=True)
    i1 = jnp.min(jnp.where(e1 == top1, erow, N_EXPERTS), axis=0, keepdims=True)
    e2 = jnp.where(erow == i1, -jnp.inf, e1)
    top2 = jnp.max(e2, axis=0, keepdims=True)
    i2 = jnp.min(jnp.where(e2 == top2, erow, N_EXPERTS), axis=0, keepdims=True)
    r = jnp.exp(top2 - top1)
    w1 = g_gate / (1.0 + r)
    w2 = g_gate * r / (1.0 + r)

    oh = jnp.where(erow == i1, 1.0, jnp.where(erow == i2, 1.0, 0.0))
    nch = (jnp.sum(oh, axis=1, keepdims=True).astype(jnp.int32) + (MOE_CH - 1)) // MOE_CH
    lr = lax.broadcasted_iota(jnp.int32, (N_EXPERTS, N_EXPERTS), 0)
    lc = lax.broadcasted_iota(jnp.int32, (N_EXPERTS, N_EXPERTS), 1)
    before_e = jnp.where(lc < lr, 1.0, 0.0).astype(BF16)
    nch_b = jnp.broadcast_to(nch.astype(F32), (N_EXPERTS, LANES)).astype(BF16)
    off_rows = jnp.dot(before_e, nch_b, preferred_element_type=F32)[:, 0:1] * MOE_CH
    tr = lax.broadcasted_iota(jnp.int32, (tb, tb), 0)
    tc = lax.broadcasted_iota(jnp.int32, (tb, tb), 1)
    before_t = jnp.where(tr < tc, 1.0, 0.0).astype(BF16)
    rank = jnp.dot(oh.astype(BF16), before_t, preferred_element_type=F32)
    row_of = off_rows + rank
    p1 = jnp.sum(jnp.where(erow == i1, row_of, 0.0), axis=0, keepdims=True).astype(jnp.int32)
    p2 = jnp.sum(jnp.where(erow == i2, row_of, 0.0), axis=0, keepdims=True).astype(jnp.int32)

    rowi = lax.broadcasted_iota(jnp.int32, (nr, tb), 0)
    hit1 = rowi == p1
    hit2 = rowi == p2
    pm = jnp.where(hit1, 1.0, jnp.where(hit2, 1.0, 0.0)).astype(BF16)
    cw = 512
    for c0 in range(0, d, cw):
        srt_ref[:, c0:c0 + cw] = jnp.dot(pm, hb[:, c0:c0 + cw], preferred_element_type=F32).astype(BF16)
    wcol = jnp.sum(jnp.where(hit1, w1, jnp.where(hit2, w2, 0.0)), axis=1, keepdims=True)
    rnd = lambda v: v.astype(BF16).astype(F32)
    hi = rnd(wcol)
    mid = rnd(wcol - hi)
    lo = (wcol - hi) - mid
    lane = lax.broadcasted_iota(jnp.int32, (nr, LANES), 1)
    extra = jnp.where(lane == 0, hi, jnp.where(lane == 1, mid, jnp.where(lane == 2, lo, 0.0)))
    srt_ref[:, d:] = extra.astype(BF16)
    prow = lax.broadcasted_iota(jnp.int32, (8, tb), 0)
    pos_ref[...] = jnp.where(prow == 0, p1, jnp.where(prow == 1, p2, 0))
    nch_ref[...] = jnp.broadcast_to(nch, (N_EXPERTS, LANES))


def _route_sort_prompt(x, g, m4, layer, w_grp, b_grp, w_exp, b_exp, seq):
    t, d = x.shape
    nblk = t // MOE_TB
    tpb = seq // MOE_TB
    pad = jnp.zeros((8 - N_GROUPS, d), F32)
    wt = jnp.concatenate([w_grp.T, pad, w_exp.T], axis=0)
    bt = jnp.concatenate([b_grp, jnp.zeros((8 - N_GROUPS,), F32), b_exp]).reshape(MOE_ROUTE_ROWS, 1)
    full = lambda shape: pl.BlockSpec(shape, lambda i: (0,) * len(shape))
    return pl.pallas_call(
        _route_sort_kernel,
        grid=(nblk,),
        in_specs=[pl.BlockSpec((MOE_TB, d), lambda i: (i, 0)), full((1, d)),
                  _mod_spec(layer, 3, tpb, d), _mod_spec(layer, 4, tpb, d),
                  full((MOE_ROUTE_ROWS, d)), full((MOE_ROUTE_ROWS, 1))],
        out_specs=(pl.BlockSpec((MOE_NR, d + LANES), lambda i: (i, 0)),
                   pl.BlockSpec((None, 8, MOE_TB), lambda i: (i, 0, 0)),
                   pl.BlockSpec((None, N_EXPERTS, LANES), lambda i: (i, 0, 0))),
        out_shape=(jax.ShapeDtypeStruct((nblk * MOE_NR, d + LANES), BF16),
                   jax.ShapeDtypeStruct((nblk, 8, MOE_TB), jnp.int32),
                   jax.ShapeDtypeStruct((nblk, N_EXPERTS, LANES), jnp.int32)),
        compiler_params=_cparams(("arbitrary",)),
        name="moe_route_sort",
    )(x, g.reshape(1, d), m4, m4, wt, bt)


def _moe_n_tiles(nblk):
    return (nblk * (MOE_NCH - 1)) // MOE_CPT + N_EXPERTS


def _moe_schedule(nch):
    nblk = nch.shape[0]
    n_tiles = _moe_n_tiles(nblk)
    off_c = jnp.cumsum(nch, axis=1) - nch
    n_e = nch.sum(axis=0)
    tiles_e = (n_e + MOE_CPT - 1) // MOE_CPT
    tile_end = jnp.cumsum(tiles_e)
    tile_start = tile_end - tiles_e
    tau = jnp.arange(n_tiles, dtype=jnp.int32)
    e = jnp.minimum((tau[:, None] >= tile_end[None, :]).sum(axis=1), N_EXPERTS - 1).astype(jnp.int32)
    g = (tau - tile_start[e])[:, None] * MOE_CPT + jnp.arange(MOE_CPT, dtype=jnp.int32)[None, :]
    valid = (tau < tile_end[-1])[:, None] & (g < n_e[e][:, None])
    cum_b = jnp.cumsum(nch, axis=0).T[e]
    b = jnp.minimum((g[:, :, None] >= cum_b[:, None, :]).sum(axis=-1), nblk - 1)
    e2 = jnp.broadcast_to(e[:, None], b.shape)
    in_block = g - (jnp.take_along_axis(cum_b, b, axis=1) - nch[b, e2])
    chunk = b * MOE_NCH + off_c[b, e2] + in_block
    src = jnp.where(valid, chunk, MOE_NCH - 1)
    spare = nblk * MOE_NCH + (tau % 2)[:, None] * MOE_CPT + jnp.arange(MOE_CPT, dtype=jnp.int32)[None, :]
    dst = jnp.where(valid, chunk, spare)
    return e, src.reshape(-1).astype(jnp.int32), dst.reshape(-1).astype(jnp.int32)


def _moe_expert_kernel(te_ref, src_ref, dst_ref, srt_hbm, w1_ref, w3_ref, w2_ref, yin_hbm, y_hbm,
                       xbuf, ybuf, w1b, w3b, w2b, sem_in, sem_out, *, n_tiles, d):
    del yin_hbm
    t = pl.program_id(0)
    slot = t % 2

    def in_copy(tile, s, c):
        row = pl.multiple_of(src_ref[tile * MOE_CPT + c] * MOE_CH, MOE_CH)
        return pltpu.make_async_copy(srt_hbm.at[pl.ds(row, MOE_CH), :], xbuf.at[s, pl.ds(c * MOE_CH, MOE_CH), :],
                                     sem_in.at[s])

    def out_copy(tile, s, c):
        row = pl.multiple_of(dst_ref[tile * MOE_CPT + c] * MOE_CH, MOE_CH)
        return pltpu.make_async_copy(ybuf.at[s, pl.ds(c * MOE_CH, MOE_CH), :], y_hbm.at[pl.ds(row, MOE_CH), :],
                                     sem_out.at[s])

    @pl.when(t == 0)
    def _():
        for c in range(MOE_CPT):
            in_copy(0, 0, c).start()

    @pl.when(t + 1 < n_tiles)
    def _():
        for c in range(MOE_CPT):
            in_copy(t + 1, 1 - slot, c).start()

    for c in range(MOE_CPT):
        in_copy(t, slot, c).wait()

    @pl.when(t >= 2)
    def _():
        for c in range(MOE_CPT):
            out_copy(t - 2, slot, c).wait()

    @pl.when((t == 0) | (te_ref[t] != te_ref[jnp.maximum(t - 1, 0)]))
    def _():
        w1b[...] = w1_ref[...].astype(BF16)
        w3b[...] = w3_ref[...].astype(BF16)
        w2b[...] = w2_ref[...].astype(BF16)

    x = xbuf[slot]
    h = x[:, :d]
    wcol = jnp.sum(x[:, d:].astype(F32), axis=1, keepdims=True)
    a = jnp.dot(h, w1b[...], preferred_element_type=F32)
    b = jnp.dot(h, w3b[...], preferred_element_type=F32)
    hid = (_silu(a) * b * wcol).astype(BF16)
    ybuf[slot] = jnp.dot(hid, w2b[...], preferred_element_type=F32).astype(BF16)

    for c in range(MOE_CPT):
        out_copy(t, slot, c).start()

    @pl.when(t == n_tiles - 1)
    def _():
        for c in range(MOE_CPT):
            out_copy(t, slot, c).wait()
        if n_tiles >= 2:
            for c in range(MOE_CPT):
                out_copy(t - 1, 1 - slot, c).wait()


def _moe_experts(srt, te, src, dst, w1, w3, w2, layer, nblk):
    d = srt.shape[1] - LANES
    f = w1.shape[-1]
    n_tiles = _moe_n_tiles(nblk)
    tile_rows = MOE_CPT * MOE_CH
    y_rows = nblk * MOE_NR + 2 * tile_rows
    wspec = lambda shape: pl.BlockSpec((None, None) + shape, lambda t, te, src, dst: (layer, te[t], 0, 0))
    return pl.pallas_call(
        functools.partial(_moe_expert_kernel, n_tiles=n_tiles, d=d),
        grid_spec=pltpu.PrefetchScalarGridSpec(
            num_scalar_prefetch=3,
            grid=(n_tiles,),
            in_specs=[pl.BlockSpec(memory_space=pl.ANY), wspec((d, f)), wspec((d, f)), wspec((f, d)),
                      pl.BlockSpec(memory_space=pl.ANY)],
            out_specs=pl.BlockSpec(memory_space=pl.ANY),
            scratch_shapes=[pltpu.VMEM((2, tile_rows, d + LANES), BF16), pltpu.VMEM((2, tile_rows, d), BF16),
                            pltpu.VMEM((d, f), BF16), pltpu.VMEM((d, f), BF16), pltpu.VMEM((f, d), BF16),
                            pltpu.SemaphoreType.DMA((2,)), pltpu.SemaphoreType.DMA((2,))]),
        out_shape=jax.ShapeDtypeStruct((y_rows, d), BF16),
        input_output_aliases={7: 0},
        compiler_params=_cparams(("arbitrary",)),
        name="moe_experts",
    )(te, src, dst, srt, w1, w3, w2, jnp.zeros((y_rows, d), BF16))


def _unsort_kernel(y_ref, pos_ref, x_ref, gate_ref, o_ref):
    nr = y_ref.shape[0]
    tb, d = x_ref.shape
    rowi = lax.broadcasted_iota(jnp.int32, (nr, tb), 0)
    pm = jnp.where(rowi == pos_ref[0:1, :], 1.0, jnp.where(rowi == pos_ref[1:2, :], 1.0, 0.0)).astype(BF16)
    cw = 512
    for c0 in range(0, d, cw):
        y = lax.dot_general(pm, y_ref[:, c0:c0 + cw], (((0,), (0,)), ((), ())), preferred_element_type=F32)
        o_ref[:, c0:c0 + cw] = x_ref[:, c0:c0 + cw] + gate_ref[:, c0:c0 + cw] * y


def _moe_unsort(y_srt, pos, x, m4, layer, seq):
    t, d = x.shape
    tpb = seq // MOE_TB
    return pl.pallas_call(
        _unsort_kernel,
        grid=(t // MOE_TB,),
        in_specs=[pl.BlockSpec((MOE_NR, d), lambda i: (i, 0)),
                  pl.BlockSpec((None, 8, MOE_TB), lambda i: (i, 0, 0)),
                  pl.BlockSpec((MOE_TB, d), lambda i: (i, 0)),
                  _mod_spec(layer, 5, tpb, d)],
        out_specs=pl.BlockSpec((MOE_TB, d), lambda i: (i, 0)),
        out_shape=jax.ShapeDtypeStruct((t, d), F32),
        compiler_params=_cparams(("arbitrary",)),
        name="moe_unsort",
    )(y_srt, pos, x, m4)


def _s_in_kernel(*refs, act, has_bias):
    if has_bias:
        x_ref, g_ref, sh_ref, sc_ref, w_ref, b_ref, o_ref = refs
    else:
        x_ref, g_ref, sh_ref, sc_ref, w_ref, o_ref = refs
    h = _rmsnorm(x_ref[...], g_ref[...]) * (1.0 + sc_ref[...]) + sh_ref[...]
    y = _dot_lp(h, w_ref[...])
    if has_bias:
        y = y + b_ref[...]
    if act == "gelu":
        y = _gelu_tanh(y)
    elif act == "logsig":
        y = _log_sigmoid(y)
    o_ref[...] = y


def _s_in(x, g, sh, sc, w, bias, act, tn):
    m, d = x.shape
    n = w.shape[1]
    tn = min(tn, n)
    full = pl.BlockSpec((m, d), lambda j: (0, 0))
    ins = [x, g.reshape(1, d), sh, sc, w]
    in_specs = [full, pl.BlockSpec((1, d), lambda j: (0, 0)), full, full, pl.BlockSpec((d, tn), lambda j: (0, j))]
    if bias is not None:
        ins.append(bias.reshape(1, n))
        in_specs.append(pl.BlockSpec((1, tn), lambda j: (0, j)))
    return pl.pallas_call(
        functools.partial(_s_in_kernel, act=act, has_bias=bias is not None),
        grid=(n // tn,),
        in_specs=in_specs,
        out_specs=pl.BlockSpec((m, tn), lambda j: (0, j)),
        out_shape=jax.ShapeDtypeStruct((m, n), F32),
        compiler_params=_cparams(("arbitrary",)),
        name="s_in_" + act,
    )(*ins)


def _s_out_kernel(*refs, has_bias):
    if has_bias:
        a_ref, w_ref, b_ref, x_ref, gate_ref, o_ref = refs
    else:
        a_ref, w_ref, x_ref, gate_ref, o_ref = refs
    y = _dot_lp(a_ref[...], w_ref[...])
    if has_bias:
        y = y + b_ref[...]
    o_ref[...] = x_ref[...] + gate_ref[...] * y


def _s_out(a, w, bias, x, gate, tn=512):
    m, k = a.shape
    n = w.shape[1]
    col = pl.BlockSpec((m, tn), lambda j: (0, j))
    ins = [a, w]
    in_specs = [pl.BlockSpec((m, k), lambda j: (0, 0)), pl.BlockSpec((k, tn), lambda j: (0, j))]
    if bias is not None:
        ins.append(bias.reshape(1, n))
        in_specs.append(pl.BlockSpec((1, tn), lambda j: (0, j)))
    ins += [x, gate]
    in_specs += [col, col]
    return pl.pallas_call(
        functools.partial(_s_out_kernel, has_bias=bias is not None),
        grid=(n // tn,),
        in_specs=in_specs, out_specs=col,
        out_shape=jax.ShapeDtypeStruct((m, n), F32),
        compiler_params=_cparams(("arbitrary",)),
        name="s_out",
    )(*ins)


def _s_sgu_kernel(z_ref, lng_ref, lnb_ref, w00_ref, b0_ref, gv_ref, o_ref, *, width):
    u = z_ref[:, :width]
    v = z_ref[:, width:]
    mu = jnp.mean(v, axis=-1, keepdims=True)
    vc = v - mu
    vn = vc * lax.rsqrt(jnp.mean(vc * vc, axis=-1, keepdims=True) + LN_EPS) * lng_ref[...] + lnb_ref[...]
    gv_ref[...] = vn
    o_ref[...] = u * (vn * w00_ref[...] + b0_ref[...])


def _s_sgu(z, ln_g, ln_b, ws, bs):
    m, n2 = z.shape
    width = n2 // 2
    gd = width // GM_GROUPS
    w00 = jnp.repeat(ws[:, 0, 0], gd).reshape(1, width)
    b0 = jnp.repeat(bs[:, 0], gd).reshape(1, width)
    vec = pl.BlockSpec((1, width), lambda: (0, 0))
    blk = pl.BlockSpec((m, width), lambda: (0, 0))
    return pl.pallas_call(
        functools.partial(_s_sgu_kernel, width=width),
        in_specs=[pl.BlockSpec((m, n2), lambda: (0, 0)), vec, vec, vec, vec],
        out_specs=(blk, blk),
        out_shape=(jax.ShapeDtypeStruct((m, width), F32), jax.ShapeDtypeStruct((m, width), F32)),
        name="s_sgu",
    )(z, ln_g.reshape(1, width), ln_b.reshape(1, width), w00, b0)


def _s_attn_kernel(pt_ref, q_ref, kn_ref, vn_ref, lfn_ref, cnf_ref, *refs, n_steps, pps, page, nh, scale):
    del pt_ref
    ck_refs, cv_refs, clf_refs = refs[:pps], refs[pps:2 * pps], refs[2 * pps:3 * pps]
    o_ref, m_ref, l_ref, acc_ref, carry_ref = refs[3 * pps:]
    p = pl.program_id(1)
    pflat = page * nh
    flat = pps * pflat
    lanes = m_ref.shape[1]

    @pl.when(p == 0)
    def _():
        m_ref[...] = jnp.full_like(m_ref, NEG_INF)
        l_ref[...] = jnp.zeros_like(l_ref)
        acc_ref[...] = jnp.zeros_like(acc_ref)
        carry_ref[...] = jnp.zeros_like(carry_ref)

    def per_head(x, op):
        y = x[:, :lanes]
        for i in range(1, flat // lanes):
            y = op(y, x[:, i * lanes:(i + 1) * lanes])
        s = nh
        while s < lanes:
            y = op(y, pltpu.roll(y, s, axis=1))
            s *= 2
        return y

    tile = lambda y: jnp.concatenate([y] * (flat // lanes), axis=1)

    def to_col(row):
        r = lax.broadcasted_iota(jnp.int32, (nh, nh), 0)
        c = lax.broadcasted_iota(jnp.int32, (nh, nh), 1)
        return jnp.sum(jnp.where(r == c, jnp.broadcast_to(row, (nh, nh)), 0.0), axis=1, keepdims=True)

    qb = q_ref[...].astype(BF16)
    own_head = (lax.broadcasted_iota(jnp.int32, (nh, pflat), 1) % nh) == lax.broadcasted_iota(jnp.int32, (nh, pflat), 0)

    lf = jnp.concatenate([r[...] for r in clf_refs], axis=1)
    lane_idx = lax.broadcasted_iota(jnp.int32, (1, flat), 1)
    inc = lf
    step = nh
    while step < flat:
        if step % lanes == 0:
            shifted = jnp.concatenate([inc[:, step:], jnp.zeros((1, step), F32)], axis=1)
        else:
            shifted = jnp.where(lane_idx < flat - step, pltpu.roll(inc, flat - step, axis=1), 0.0)
        inc = inc + shifted
        step *= 2
    rest = (inc - lf) + tile(carry_ref[...])
    carry_ref[...] = carry_ref[...] + per_head(lf, jnp.add)

    def own_logits(k_ref):
        sf = lax.dot_general(qb, k_ref[...].astype(BF16), (((1,), (1,)), ((), ())), preferred_element_type=F32)
        return jnp.sum(jnp.where(own_head, sf, 0.0), axis=0, keepdims=True)

    qk = jnp.concatenate([own_logits(r) for r in ck_refs], axis=1)
    s = qk * scale + jnp.concatenate([cnf_ref[...]] * pps, axis=1) + rest
    m_old = m_ref[...]
    m_new = jnp.maximum(m_old, per_head(s, jnp.maximum))
    alpha = jnp.exp(m_old - m_new)
    pr = jnp.exp(s - tile(m_new))
    l_ref[...] = l_ref[...] * alpha + per_head(pr, jnp.add)
    m_ref[...] = m_new
    pv = None
    for j, v_ref in enumerate(cv_refs):
        pr_j = pr[:, j * pflat:(j + 1) * pflat]
        pm = jnp.where(own_head, jnp.broadcast_to(pr_j, (nh, pflat)), 0.0).astype(BF16)
        d = jnp.dot(pm, v_ref[...].astype(BF16), preferred_element_type=F32)
        pv = d if pv is None else pv + d
    acc_ref[...] = acc_ref[...] * to_col(alpha[:, :nh]) + pv

    @pl.when(p == n_steps - 1)
    def _():
        rnd = lambda x: x.astype(BF16).astype(F32)
        c_new = lfn_ref[...]
        sn = lax.dot_general(qb, kn_ref[...].astype(BF16), (((1,), (1,)), ((), ())), preferred_element_type=F32)
        r = lax.broadcasted_iota(jnp.int32, (nh, nh), 0)
        c = lax.broadcasted_iota(jnp.int32, (nh, nh), 1)
        s_n = jnp.sum(jnp.where(r == c, sn, 0.0), axis=0, keepdims=True) * scale + (c_new - c_new)
        m_o = m_ref[:, :nh]
        m_n = jnp.maximum(m_o, s_n)
        alpha_n = jnp.exp(m_o - m_n)
        pr_n = jnp.exp(s_n - m_n)
        l_n = l_ref[:, :nh] * alpha_n + pr_n
        acc = acc_ref[...] * to_col(alpha_n) + to_col(rnd(pr_n)) * rnd(vn_ref[...])
        o_ref[...] = acc / to_col(l_n)


def _s_attn(q, k_new, v_new, lf_new, cache_k, cache_v, cache_logf, page_table, slot):
    bsz, d = q.shape
    n_fox, n_pool, page, nh, hd = cache_k.shape
    n_pages = page_table.shape[1]
    flat = page * nh
    pps = next(c for c in (4, 2, 1) if n_pages % c == 0)
    n_steps = n_pages // pps
    heads = pl.BlockSpec((None, nh, hd), lambda b, p, pt: (b, 0, 0))
    per_seq = lambda n: pl.BlockSpec((None, 1, n), lambda b, p, pt: (b, 0, 0))

    def pg_idx(j):
        return lambda b, p, pt: (slot, pt[b * n_pages + (n_steps - 1 - p) * pps + j], 0, 0)

    cache_k = cache_k.reshape(n_fox, n_pool, flat, hd)
    cache_v = cache_v.reshape(n_fox, n_pool, flat, hd)
    clf_flat = cache_logf.reshape(n_fox, n_pool, 1, flat)
    cn_flat = jnp.tile(lf_new, (1, page)).reshape(bsz, 1, flat)
    kv_specs = [pl.BlockSpec((None, None, flat, hd), pg_idx(j)) for j in range(pps)]
    lf_specs = [pl.BlockSpec((None, None, 1, flat), pg_idx(j)) for j in range(pps)]
    out = pl.pallas_call(
        functools.partial(_s_attn_kernel, n_steps=n_steps, pps=pps, page=page, nh=nh, scale=hd ** -0.5),
        grid_spec=pltpu.PrefetchScalarGridSpec(
            num_scalar_prefetch=1,
            grid=(bsz, n_steps),
            in_specs=[heads, heads, heads, per_seq(nh), per_seq(flat)] + kv_specs + kv_specs + lf_specs,
            out_specs=heads,
            scratch_shapes=[pltpu.VMEM((1, LANES), F32), pltpu.VMEM((1, LANES), F32), pltpu.VMEM((nh, hd), F32),
                            pltpu.VMEM((1, LANES), F32)]),
        out_shape=jax.ShapeDtypeStruct((bsz, nh, hd), F32),
        compiler_params=_cparams(("arbitrary", "arbitrary")),
        name="s_attn",
    )(page_table.reshape(-1), q.reshape(bsz, nh, hd), k_new.reshape(bsz, nh, hd), v_new.reshape(bsz, nh, hd),
      lf_new.reshape(bsz, 1, nh), cn_flat, *([cache_k] * pps), *([cache_v] * pps), *([clf_flat] * pps))
    return out.reshape(bsz, d)


def _s_moe_kernel(eid_ref, h_ref, wts_ref, w1_ref, w3_ref, w2_ref, x_ref, gate_ref, o_ref, acc_ref, *, n_pairs):
    del eid_ref
    p = pl.program_id(0)

    @pl.when(p == 0)
    def _():
        acc_ref[...] = jnp.zeros_like(acc_ref)

    h = h_ref[...]
    m = h.shape[0]
    a = _dot_lp(h, w1_ref[...])
    b = _dot_lp(h, w3_ref[...])
    wts = wts_ref[...]
    wcol = jnp.where(p % 2 == 0, wts[:, 0:1], wts[:, 1:2])
    wcol = jnp.where(lax.broadcasted_iota(jnp.int32, (m, 1), 0) == p // 2, wcol, 0.0)
    acc_ref[...] += _dot_lp(_silu(a) * b * wcol, w2_ref[...])

    @pl.when(p == n_pairs - 1)
    def _():
        o_ref[...] = x_ref[...] + gate_ref[...] * acc_ref[...]


def _s_moe(h, ids, wts, w1, w3, w2, x, gate, layer):
    m, d = h.shape
    f = w1.shape[-1]
    n_pairs = 2 * m
    full = lambda shape: pl.BlockSpec(shape, lambda p, eid: (0,) * len(shape))
    return pl.pallas_call(
        functools.partial(_s_moe_kernel, n_pairs=n_pairs),
        grid_spec=pltpu.PrefetchScalarGridSpec(
            num_scalar_prefetch=1,
            grid=(n_pairs,),
            in_specs=[full((m, d)), full((m, 2)),
                      pl.BlockSpec((None, None, d, f), lambda p, eid: (layer, eid[p], 0, 0)),
                      pl.BlockSpec((None, None, d, f), lambda p, eid: (layer, eid[p], 0, 0)),
                      pl.BlockSpec((None, None, f, d), lambda p, eid: (layer, eid[p], 0, 0)),
                      full((m, d)), full((m, d))],
            out_specs=full((m, d)),
            scratch_shapes=[pltpu.VMEM((m, d), F32)]),
        out_shape=jax.ShapeDtypeStruct((m, d), F32),
        compiler_params=_cparams(("arbitrary",)),
        name="s_moe",
    )(ids.reshape(-1), h, wts, w1, w3, w2, x, gate)


def kernel(x_prompt, x_sample, cache_k, cache_v, cache_logf, page_table, c_prompt, c_sample,
           w_ada, b_ada, norm1_g, norm2_g, final_g,
           gm_w_in, gm_b_in, gm_ln_g, gm_ln_b, gm_ws, gm_bs, gm_w_out, gm_b_out,
           fox_w_qkv, fox_w_f, fox_b_f, fox_w_o,
           moe_w_grp, moe_b_grp, moe_w_exp, moe_b_exp, moe_w1, moe_w3, moe_w2):
    bsz, seq, d = x_prompt.shape
    dbsz = x_sample.shape[0]
    depth = w_ada.shape[0]
    nh, hd = cache_k.shape[3], cache_k.shape[4]
    t = bsz * seq
    assert bsz + dbsz <= ADA_ROWS and x_sample.shape[1] == 1

    c16 = jnp.concatenate([c_prompt, c_sample, jnp.zeros((ADA_ROWS - bsz - dbsz, d), F32)], axis=0)
    m_all = _ada_all(c16, w_ada, b_ada)
    m4 = m_all.reshape(depth, ADA_ROWS, 1, 6 * d)

    xp = x_prompt.reshape(t, d)
    xs = x_sample.reshape(dbsz, d)
    k_p, v_p, lf_p, k_s, v_s, lf_s, gv_s = [], [], [], [], [], [], []
    tm = 512
    tpb = seq // tm

    for i in range(depth):
        slot = i // 2
        ms = m_all[i, bsz:bsz + dbsz].reshape(dbsz, 6, d)
        if i % 2 == 0:
            hp = _norm_mod_prompt(xp, norm1_g[i], m4, i, 0, seq)
            z = _mm_prompt("gelu", hp, gm_w_in[slot], bias=gm_b_in[slot])
            gated = _sgu_prompt(z, gm_ln_g[slot], gm_ln_b[slot], gm_ws[slot], gm_bs[slot])
            xp = _mm_prompt("resid", gated, gm_w_out[slot], bias=gm_b_out[slot], x=xp, m4=m4, layer=i, gate_slot=2,
                            seq=seq)
            zs = _s_in(xs, norm1_g[i], ms[:, 0], ms[:, 1], gm_w_in[slot], gm_b_in[slot], "gelu", 512)
            gv, gs = _s_sgu(zs, gm_ln_g[slot], gm_ln_b[slot], gm_ws[slot], gm_bs[slot])
            gv_s.append(gv)
            xs = _s_out(gs, gm_w_out[slot], gm_b_out[slot], xs, ms[:, 2])
        else:
            hp = _norm_mod_prompt(xp, norm1_g[i], m4, i, 0, seq)
            wqkv = fox_w_qkv[slot]
            q = _mm_prompt("proj_bf16", hp, wqkv, col0=0, n=d, out_scale=hd ** -0.5)
            kf, kb = _mm_prompt("proj_both", hp, wqkv, col0=d, n=d)
            vf, vb = _mm_prompt("proj_both", hp, wqkv, col0=2 * d, n=d)
            lf, qa, ka = _logf_prompt(hp, fox_w_f[slot], fox_b_f[slot], bsz, seq, hd)
            b3 = lambda a: a.reshape(bsz, seq, d)
            o = _fox_attn_prompt(b3(q), b3(qa), b3(kb), b3(ka), b3(vb), bsz, seq, nh, hd)
            xp = _mm_prompt("resid", o.reshape(t, d), fox_w_o[slot], bias=jnp.zeros((d,), F32), x=xp, m4=m4, layer=i,
                            gate_slot=2, seq=seq)
            k_p.append(kf.reshape(bsz, seq, nh, hd))
            v_p.append(vf.reshape(bsz, seq, nh, hd))
            lf_p.append(lf.reshape(bsz, seq, nh))
            qkv = _s_in(xs, norm1_g[i], ms[:, 0], ms[:, 1], wqkv, None, "none", 512)
            lfn = _s_in(xs, norm1_g[i], ms[:, 0], ms[:, 1], fox_w_f[slot], fox_b_f[slot], "logsig", 512)
            qs, kn, vn = qkv[:, :d], qkv[:, d:2 * d], qkv[:, 2 * d:]
            o_s = _s_attn(qs, kn, vn, lfn, cache_k, cache_v, cache_logf, page_table, slot)
            xs = _s_out(o_s, fox_w_o[slot], None, xs, ms[:, 2])
            k_s.append(kn.reshape(dbsz, 1, nh, hd))
            v_s.append(vn.reshape(dbsz, 1, nh, hd))
            lf_s.append(lfn.reshape(dbsz, 1, nh))

        srt, pos, nch = _route_sort_prompt(xp, norm2_g[i], m4, i, moe_w_grp[i], moe_b_grp[i], moe_w_exp[i],
                                           moe_b_exp[i], seq)
        te, src, dst = _moe_schedule(nch[:, :, 0])
        y_srt = _moe_experts(srt, te, src, dst, moe_w1, moe_w3, moe_w2, i, t // MOE_TB)
        xp = _moe_unsort(y_srt, pos, xp, m4, i, seq)
        full8 = pl.BlockSpec((dbsz, d), lambda j: (0, 0))
        h2s, _, ids, wts = _router(xs, norm2_g[i], full8, full8, (ms[:, 3], ms[:, 4]), moe_w_grp[i], moe_b_grp[i],
                                   moe_w_exp[i], moe_b_exp[i], dbsz, F32)
        xs = _s_moe(h2s, ids, wts, moe_w1, moe_w3, moe_w2, xs, ms[:, 5], i)

    y_prompt = _final_norm(xp, final_g, tm).reshape(bsz, seq, d)
    y_sample = _final_norm(xs, final_g, dbsz).reshape(dbsz, 1, d)
    return (y_prompt, y_sample, jnp.stack(k_p), jnp.stack(v_p), jnp.stack(lf_p),
            jnp.stack(k_s), jnp.stack(v_s), jnp.stack(lf_s), jnp.stack(gv_s).reshape(len(gv_s), dbsz, 1, d))
```

```python
import functools
import math

import jax
import jax.numpy as jnp
from jax import lax
from jax.experimental import pallas as pl
from jax.experimental.pallas import tpu as pltpu

F32 = jnp.float32
BF16 = jnp.bfloat16

RMS_EPS = 1e-6
LN_EPS = 1e-5
NEG_INF = -1e30
CHUNK = 128
GM_GROUPS = 16
N_GROUPS = 4
EXP_PER_GROUP = 4
N_EXPERTS = N_GROUPS * EXP_PER_GROUP
LANES = 128
BF16_ROWS = 16
ADA_ROWS = 16

V7X_VMEM_LIMIT = 56 * 1024 * 1024


def _cparams(sem, vmem=V7X_VMEM_LIMIT):
    return pltpu.CompilerParams(dimension_semantics=sem, vmem_limit_bytes=vmem)


def _silu(x):
    return x / (1.0 + jnp.exp(-x))


def _gelu_tanh(x):
    return 0.5 * x * (1.0 + jnp.tanh(math.sqrt(2.0 / math.pi) * (x + 0.044715 * (x * x * x))))


def _log_sigmoid(x):
    y = -x
    return -(jnp.maximum(y, 0.0) + jnp.log1p(jnp.exp(-jnp.abs(y))))


def _rmsnorm(x, g):
    return x * lax.rsqrt(jnp.mean(x * x, axis=-1, keepdims=True) + RMS_EPS) * g


def _stack_split(a, terms):
    parts = []
    r = a
    for _ in range(terms):
        p = r.astype(BF16).astype(F32)
        parts.append(p)
        r = r - p
    return jnp.concatenate(parts, axis=0).astype(BF16)


def _fold_rows(r, m, terms):
    out = r[:m]
    for t in range(1, terms):
        out = out + r[t * m:(t + 1) * m]
    return out


def _dot_lp(a, b):
    m = a.shape[0]
    pad = (-m) % BF16_ROWS
    if pad:
        a = jnp.concatenate([a, jnp.zeros((pad, a.shape[1]), F32)], axis=0)
    r = jnp.dot(a.astype(BF16), b.astype(BF16), preferred_element_type=F32)
    return r[:m] if pad else r


def _dot_split_exact_rhs(a, b_bf16, terms=2):
    m = a.shape[0]
    mp = m + (-m) % BF16_ROWS
    if mp != m:
        a = jnp.concatenate([a, jnp.zeros((mp - m, a.shape[1]), F32)], axis=0)
    r = jnp.dot(_stack_split(a, terms), b_bf16, preferred_element_type=F32)
    return _fold_rows(r, mp, terms)[:m]


def _dot_exact_lhs_split(a_bf16, b, terms=3):
    out = None
    r = b
    for _ in range(terms):
        p = r.astype(BF16)
        d = jnp.dot(a_bf16, p, preferred_element_type=F32)
        out = d if out is None else out + d
        r = r - p.astype(F32)
    return out


def _ada_kernel(c_ref, w_ref, b_ref, o_ref):
    o_ref[...] = _dot_lp(_silu(c_ref[...]), w_ref[...]) + b_ref[...]


def _ada_all(c16, w_ada, b_ada, tn=1024):
    depth, d, n6 = w_ada.shape
    return pl.pallas_call(
        _ada_kernel,
        grid=(depth, n6 // tn),
        in_specs=[pl.BlockSpec((ADA_ROWS, d), lambda l, j: (0, 0)),
                  pl.BlockSpec((None, d, tn), lambda l, j: (l, 0, j)),
                  pl.BlockSpec((None, 1, tn), lambda l, j: (l, 0, j))],
        out_specs=pl.BlockSpec((None, ADA_ROWS, tn), lambda l, j: (l, 0, j)),
        out_shape=jax.ShapeDtypeStruct((depth, ADA_ROWS, n6), F32),
        compiler_params=_cparams(("arbitrary", "arbitrary")),
        name="ada_all",
    )(c16, w_ada, b_ada.reshape(depth, 1, n6))


def _norm_mod_kernel(x_ref, g_ref, sh_ref, sc_ref, o_ref):
    y = _rmsnorm(x_ref[...], g_ref[...])
    o_ref[...] = (y * (1.0 + sc_ref[...]) + sh_ref[...]).astype(o_ref.dtype)


def _norm_kernel(x_ref, g_ref, o_ref):
    o_ref[...] = _rmsnorm(x_ref[...], g_ref[...]).astype(o_ref.dtype)


def _mod_spec(layer, slot, tiles_per_batch, d):
    return pl.BlockSpec((None, None, 1, d), lambda i, *_: (layer, i // tiles_per_batch, 0, slot))


def _norm_mod_prompt(x, g, m4, layer, slot_shift, seq, tm=512):
    t, d = x.shape
    tpb = seq // tm
    return pl.pallas_call(
        _norm_mod_kernel,
        grid=(t // tm,),
        in_specs=[pl.BlockSpec((tm, d), lambda i: (i, 0)),
                  pl.BlockSpec((1, d), lambda i: (0, 0)),
                  _mod_spec(layer, slot_shift, tpb, d),
                  _mod_spec(layer, slot_shift + 1, tpb, d)],
        out_specs=pl.BlockSpec((tm, d), lambda i: (i, 0)),
        out_shape=jax.ShapeDtypeStruct((t, d), BF16),
        compiler_params=_cparams(("arbitrary",)),
        name="norm_mod_prompt",
    )(x, g.reshape(1, d), m4, m4)


def _final_norm(x, g, tm):
    t, d = x.shape
    return pl.pallas_call(
        _norm_kernel,
        grid=(t // tm,),
        in_specs=[pl.BlockSpec((tm, d), lambda i: (i, 0)), pl.BlockSpec((1, d), lambda i: (0, 0))],
        out_specs=pl.BlockSpec((tm, d), lambda i: (i, 0)),
        out_shape=jax.ShapeDtypeStruct((t, d), F32),
        compiler_params=_cparams(("arbitrary",)),
        name="final_norm",
    )(x, g.reshape(1, d))


def _mm_kernel(*refs, kind, out_scale):
    if kind == "gelu":
        a_ref, w_ref, b_ref, z_ref, wb_ref = refs
    elif kind == "resid":
        a_ref, w_ref, b_ref, x_ref, gate_ref, xo_ref, wb_ref = refs
    elif kind == "proj_bf16":
        a_ref, w_ref, ob_ref, wb_ref = refs
    else:
        a_ref, w_ref, of_ref, ob_ref, wb_ref = refs

    @pl.when(pl.program_id(1) == 0)
    def _():
        wb_ref[...] = w_ref[...].astype(BF16)

    acc = jnp.dot(a_ref[...], wb_ref[...], preferred_element_type=F32)
    if kind == "gelu":
        z_ref[...] = _gelu_tanh(acc + b_ref[...]).astype(z_ref.dtype)
    elif kind == "resid":
        xo_ref[...] = x_ref[...] + gate_ref[...] * (acc + b_ref[...])
    elif kind == "proj_bf16":
        ob_ref[...] = (acc * out_scale).astype(BF16)
    else:
        of_ref[...] = acc
        ob_ref[...] = acc.astype(BF16)


def _mm_prompt(kind, a, w, *, col0=0, n=None, bias=None, x=None, m4=None, layer=None, gate_slot=None, seq=None,
               out_scale=1.0,
               tm=512, tn=1024):
    t, k = a.shape
    n = w.shape[1] if n is None else n
    jb = col0 // tn
    a_spec = pl.BlockSpec((tm, k), lambda j, i: (i, 0))
    w_spec = pl.BlockSpec((k, tn), lambda j, i: (0, j + jb))
    o_spec = pl.BlockSpec((tm, tn), lambda j, i: (i, j))
    ins, in_specs = [a, w], [a_spec, w_spec]
    if kind in ("gelu", "resid"):
        ins.append(bias.reshape(1, n))
        in_specs.append(pl.BlockSpec((1, tn), lambda j, i: (0, j)))
    if kind == "resid":
        tpb = seq // tm
        ins += [x, m4]
        in_specs += [o_spec, pl.BlockSpec((None, None, 1, tn),
                                          lambda j, i: (layer, i // tpb, 0, gate_slot * (n // tn) + j))]
    if kind == "gelu":
        out_shape, out_specs = jax.ShapeDtypeStruct((t, n), BF16), o_spec
    elif kind == "resid":
        out_shape, out_specs = jax.ShapeDtypeStruct((t, n), F32), o_spec
    elif kind == "proj_bf16":
        out_shape, out_specs = jax.ShapeDtypeStruct((t, n), BF16), o_spec
    else:
        out_shape = (jax.ShapeDtypeStruct((t, n), F32), jax.ShapeDtypeStruct((t, n), BF16))
        out_specs = (o_spec, o_spec)
    return pl.pallas_call(
        functools.partial(_mm_kernel, kind=kind, out_scale=out_scale),
        grid=(n // tn, t // tm),
        in_specs=in_specs, out_specs=out_specs, out_shape=out_shape,
        scratch_shapes=[pltpu.VMEM((k, tn), BF16)],
        compiler_params=_cparams(("arbitrary", "arbitrary")),
        name="mm_" + kind,
    )(*ins)


def _sgu_kernel(z_ref, lng_ref, lnb_ref, ws_ref, bst_ref, o_ref, wt_ref, *, width):
    @pl.when(pl.program_id(0) == 0)
    def _():
        r = lax.broadcasted_iota(jnp.int32, (CHUNK, CHUNK), 0)
        c = lax.broadcasted_iota(jnp.int32, (CHUNK, CHUNK), 1)
        for g in range(GM_GROUPS):
            wt_ref[g] = jnp.where(c <= r, ws_ref[g], 0.0).astype(BF16)

    v = z_ref[:, width:].astype(F32)
    mu = jnp.mean(v, axis=-1, keepdims=True)
    vc = v - mu
    vn = vc * lax.rsqrt(jnp.mean(vc * vc, axis=-1, keepdims=True) + LN_EPS) * lng_ref[...] + lnb_ref[...]
    vb = vn.astype(BF16)
    gd = width // GM_GROUPS
    for g in range(GM_GROUPS):
        lo, hi = g * gd, (g + 1) * gd
        s = jnp.dot(wt_ref[g], vb[:, lo:hi], preferred_element_type=F32) + bst_ref[:, g:g + 1]
        o_ref[:, lo:hi] = (z_ref[:, lo:hi].astype(F32) * s).astype(BF16)


def _sgu_prompt(z, ln_g, ln_b, ws, bs):
    t, n2 = z.shape
    width = n2 // 2
    return pl.pallas_call(
        functools.partial(_sgu_kernel, width=width),
        grid=(t // CHUNK,),
        in_specs=[pl.BlockSpec((CHUNK, n2), lambda i: (i, 0)),
                  pl.BlockSpec((1, width), lambda i: (0, 0)),
                  pl.BlockSpec((1, width), lambda i: (0, 0)),
                  pl.BlockSpec((GM_GROUPS, CHUNK, CHUNK), lambda i: (0, 0, 0)),
                  pl.BlockSpec((CHUNK, GM_GROUPS), lambda i: (0, 0))],
        out_specs=pl.BlockSpec((CHUNK, width), lambda i: (i, 0)),
        out_shape=jax.ShapeDtypeStruct((t, width), BF16),
        scratch_shapes=[pltpu.VMEM((GM_GROUPS, CHUNK, CHUNK), BF16)],
        compiler_params=_cparams(("arbitrary",)),
        name="sgu_prompt",
    )(z, ln_g.reshape(1, width), ln_b.reshape(1, width), ws, bs.T)


def _logf_kernel(h_ref, wf_ref, bf_ref, lf_ref, qa_ref, ka_ref, carry_ref, *, tm, hd):
    @pl.when(pl.program_id(1) == 0)
    def _():
        carry_ref[...] = jnp.zeros_like(carry_ref)

    logit = jnp.dot(h_ref[...], wf_ref[...].astype(BF16), preferred_element_type=F32) + bf_ref[...]
    lf = _log_sigmoid(logit)
    lf_ref[...] = lf
    r = lax.broadcasted_iota(jnp.int32, (tm, tm), 0)
    c = lax.broadcasted_iota(jnp.int32, (tm, tm), 1)
    tri = jnp.where(c <= r, 1.0, 0.0).astype(BF16)
    cum = _dot_exact_lhs_split(tri, lf, 3) + carry_ref[...]
    carry_ref[...] = cum[tm - 1:tm, :]
    lane = lax.broadcasted_iota(jnp.int32, (tm, hd), 1)
    rnd = lambda x: x.astype(BF16).astype(F32)
    for h in range(lf.shape[1]):
        c_h = cum[:, h:h + 1]
        hi = rnd(c_h)
        mid = rnd(c_h - hi)
        lo = (c_h - hi) - mid
        qa = jnp.where(lane == 0, hi, jnp.where(lane == 1, mid, jnp.where(lane == 2, lo, jnp.where(lane < 6, 1.0, 0.0))))
        ka = jnp.where(lane < 3, 1.0, jnp.where(lane == 3, -hi, jnp.where(lane == 4, -mid, jnp.where(lane == 5, -lo, 0.0))))
        qa_ref[:, h * hd:(h + 1) * hd] = qa.astype(BF16)
        ka_ref[:, h * hd:(h + 1) * hd] = ka.astype(BF16)


def _logf_prompt(h, w_f, b_f, bsz, seq, hd, tm=512):
    t, d = h.shape
    nh = w_f.shape[1]
    tpb = seq // tm
    row = lambda n: pl.BlockSpec((tm, n), lambda b, i: (b * tpb + i, 0))
    return pl.pallas_call(
        functools.partial(_logf_kernel, tm=tm, hd=hd),
        grid=(bsz, tpb),
        in_specs=[row(d), pl.BlockSpec((d, nh), lambda b, i: (0, 0)), pl.BlockSpec((1, nh), lambda b, i: (0, 0))],
        out_specs=(row(nh), row(nh * hd), row(nh * hd)),
        out_shape=(jax.ShapeDtypeStruct((t, nh), F32), jax.ShapeDtypeStruct((t, nh * hd), BF16),
                   jax.ShapeDtypeStruct((t, nh * hd), BF16)),
        scratch_shapes=[pltpu.VMEM((1, nh), F32)],
        compiler_params=_cparams(("arbitrary", "arbitrary")),
        name="logf_prompt",
    )(h, w_f, b_f.reshape(1, nh))


def _fox_attn_kernel(q_ref, qa_ref, k_ref, ka_ref, v_ref, o_ref, *, tq, hd):
    qi = pl.program_id(2)
    hps = q_ref.shape[1] // hd
    qts = [jnp.concatenate([q_ref[:, g * hd:(g + 1) * hd], qa_ref[:, g * hd:(g + 1) * hd]], axis=1) for g in range(hps)]
    ones_col = jnp.where(lax.broadcasted_iota(jnp.int32, (tq, hd), 1) == 0, 1.0, 0.0).astype(BF16)

    def block(j, carry, diagonal):
        off = pl.multiple_of(j * tq, tq)
        out = []
        for g in range(hps):
            m, acc = carry[g]
            cols = slice(g * hd, (g + 1) * hd)
            kt = jnp.concatenate([k_ref[pl.ds(off, tq), cols], ka_ref[pl.ds(off, tq), cols]], axis=1)
            vt = jnp.concatenate([v_ref[pl.ds(off, tq), cols], ones_col], axis=1)
            s = lax.dot_general(qts[g], kt, (((1,), (1,)), ((), ())), preferred_element_type=F32)
            if diagonal:
                r = lax.broadcasted_iota(jnp.int32, (tq, tq), 0)
                c = lax.broadcasted_iota(jnp.int32, (tq, tq), 1)
                s = jnp.where(c <= r, s, NEG_INF)
            m_new = jnp.maximum(m, jnp.max(s, axis=1, keepdims=True))
            alpha = jnp.exp(m - m_new)
            p = jnp.exp(s - m_new)
            acc = acc * alpha + jnp.dot(p.astype(BF16), vt, preferred_element_type=F32)
            out.append((m_new, acc))
        return tuple(out)

    init = tuple((jnp.full((tq, 1), NEG_INF, F32), jnp.zeros((tq, 2 * hd), F32)) for _ in range(hps))
    carry = lax.fori_loop(0, qi, lambda j, c: block(j, c, False), init)
    carry = block(qi, carry, True)
    for g in range(hps):
        acc = carry[g][1]
        o_ref[:, g * hd:(g + 1) * hd] = (acc[:, :hd] / acc[:, hd:hd + 1]).astype(o_ref.dtype)


def _fox_attn_prompt(q, qa, k, ka, v, bsz, seq, nh, hd, tq=512, hps=4):
    qblk = pl.BlockSpec((None, tq, hps * hd), lambda b, h, i: (b, i, h))
    kblk = pl.BlockSpec((None, seq, hps * hd), lambda b, h, i: (b, 0, h))
    return pl.pallas_call(
        functools.partial(_fox_attn_kernel, tq=tq, hd=hd),
        grid=(bsz, nh // hps, seq // tq),
        in_specs=[qblk, qblk, kblk, kblk, kblk],
        out_specs=qblk,
        out_shape=jax.ShapeDtypeStruct((bsz, seq, nh * hd), BF16),
        compiler_params=_cparams(("arbitrary", "arbitrary", "arbitrary")),
        name="fox_attn_prompt",
    )(q, qa, k, ka, v)


def _route(g_logits, e_logits):
    m = g_logits.shape[0]
    gi = lax.broadcasted_iota(jnp.int32, (m, N_GROUPS), 1)
    ei = lax.broadcasted_iota(jnp.int32, (m, N_EXPERTS), 1)
    gmax = jnp.max(g_logits, axis=1, keepdims=True)
    g_idx = jnp.min(jnp.where(g_logits == gmax, gi, N_GROUPS), axis=1, keepdims=True)
    g_gate = 1.0 / jnp.sum(jnp.exp(g_logits - gmax), axis=1, keepdims=True)
    in_group = (ei // EXP_PER_GROUP) == g_idx
    e1 = jnp.where(in_group, e_logits, -jnp.inf)
    top1 = jnp.max(e1, axis=1, keepdims=True)
    i1 = jnp.min(jnp.where(e1 == top1, ei, N_EXPERTS), axis=1, keepdims=True)
    e2 = jnp.where(ei == i1, -jnp.inf, e1)
    top2 = jnp.max(e2, axis=1, keepdims=True)
    i2 = jnp.min(jnp.where(e2 == top2, ei, N_EXPERTS), axis=1, keepdims=True)
    r = jnp.exp(top2 - top1)
    w1 = g_gate / (1.0 + r)
    w2 = g_gate * r / (1.0 + r)
    comb = jnp.where(ei == i1, w1, 0.0) + jnp.where(ei == i2, w2, 0.0)
    two = lax.broadcasted_iota(jnp.int32, (m, 2), 1)
    ids = jnp.where(two == 0, i1, i2)
    wts = jnp.where(two == 0, w1, w2)
    return comb, ids, wts


def _router_kernel(x_ref, g_ref, sh_ref, sc_ref, wg_ref, bg_ref, we_ref, be_ref, h_ref, comb_ref, ids_ref, wts_ref):
    y = _rmsnorm(x_ref[...], g_ref[...])
    h = y * (1.0 + sc_ref[...]) + sh_ref[...]
    h_ref[...] = h.astype(h_ref.dtype)
    g_logits = _dot_lp(h, wg_ref[...]) + bg_ref[...]
    e_logits = _dot_lp(h, we_ref[...]) + be_ref[...]
    comb, ids, wts = _route(g_logits, e_logits)
    comb_ref[...] = comb
    ids_ref[...] = ids
    wts_ref[...] = wts


def _router(x, g, sh_spec, sc_spec, mod_arrays, w_grp, b_grp, w_exp, b_exp, tm, h_dtype):
    t, d = x.shape
    full = lambda shape: pl.BlockSpec(shape, lambda i: (0,) * len(shape))
    row = lambda n: pl.BlockSpec((tm, n), lambda i: (i, 0))
    return pl.pallas_call(
        _router_kernel,
        grid=(t // tm,),
        in_specs=[row(d), full((1, d)), sh_spec, sc_spec,
                  full((d, N_GROUPS)), full((1, N_GROUPS)), full((d, N_EXPERTS)), full((1, N_EXPERTS))],
        out_specs=(row(d), row(N_EXPERTS), row(2), row(2)),
        out_shape=(jax.ShapeDtypeStruct((t, d), h_dtype), jax.ShapeDtypeStruct((t, N_EXPERTS), F32),
                   jax.ShapeDtypeStruct((t, 2), jnp.int32), jax.ShapeDtypeStruct((t, 2), F32)),
        compiler_params=_cparams(("arbitrary",)),
        name="moe_router",
    )(x, g.reshape(1, d), *mod_arrays, w_grp, b_grp.reshape(1, N_GROUPS), w_exp, b_exp.reshape(1, N_EXPERTS))


MOE_TB = 512
MOE_CH = BF16_ROWS
MOE_CPT = 40
MOE_NCH = (2 * MOE_TB + N_EXPERTS * (MOE_CH - 1)) // MOE_CH + 1
MOE_NR = MOE_NCH * MOE_CH
MOE_ROUTE_ROWS = 24


def _route_sort_kernel(x_ref, g_ref, sh_ref, sc_ref, wt_ref, bt_ref, srt_ref, pos_ref, nch_ref):
    tb, d = x_ref.shape
    nr = srt_ref.shape[0]
    h = _rmsnorm(x_ref[...], g_ref[...]) * (1.0 + sc_ref[...]) + sh_ref[...]
    hb = h.astype(BF16)
    lg = lax.dot_general(wt_ref[...].astype(BF16), hb, (((1,), (1,)), ((), ())), preferred_element_type=F32) + bt_ref[...]
    grow = lax.broadcasted_iota(jnp.int32, (8, tb), 0)
    gl = jnp.where(grow < N_GROUPS, lg[0:8], -jnp.inf)
    el = lg[8:8 + N_EXPERTS]
    erow = lax.broadcasted_iota(jnp.int32, (N_EXPERTS, tb), 0)
    gmax = jnp.max(gl, axis=0, keepdims=True)
    g_idx = jnp.min(jnp.where(gl == gmax, grow, 8), axis=0, keepdims=True)
    g_gate = 1.0 / jnp.sum(jnp.exp(gl - gmax), axis=0, keepdims=True)
    e1 = jnp.where(erow // EXP_PER_GROUP == g_idx, el, -jnp.inf)
    top1 = jnp.max(e1, axis=0, keepdims=True)
    i1 = jnp.min(jnp.where(e1 == top1, erow, N_EXPERTS), axis=0, keepdims=True)
    e2 = jnp.where(erow == i1, -jnp.inf, e1)
    top2 = jnp.max(e2, axis=0, keepdims=True)
    i2 = jnp.min(jnp.where(e2 == top2, erow, N_EXPERTS), axis=0, keepdims=True)
    r = jnp.exp(top2 - top1)
    w1 = g_gate / (1.0 + r)
    w2 = g_gate * r / (1.0 + r)

    oh = jnp.where(erow == i1, 1.0, jnp.where(erow == i2, 1.0, 0.0))
    nch = (jnp.sum(oh, axis=1, keepdims=True).astype(jnp.int32) + (MOE_CH - 1)) // MOE_CH
    lr = lax.broadcasted_iota(jnp.int32, (N_EXPERTS, N_EXPERTS), 0)
    lc = lax.broadcasted_iota(jnp.int32, (N_EXPERTS, N_EXPERTS), 1)
    before_e = jnp.where(lc < lr, 1.0, 0.0).astype(BF16)
    nch_b = jnp.broadcast_to(nch.astype(F32), (N_EXPERTS, LANES)).astype(BF16)
    off_rows = jnp.dot(before_e, nch_b, preferred_element_type=F32)[:, 0:1] * MOE_CH
    tr = lax.broadcasted_iota(jnp.int32, (tb, tb), 0)
    tc = lax.broadcasted_iota(jnp.int32, (tb, tb), 1)
    before_t = jnp.where(tr < tc, 1.0, 0.0).astype(BF16)
    rank = jnp.dot(oh.astype(BF16), before_t, preferred_element_type=F32)
    row_of = off_rows + rank
    p1 = jnp.sum(jnp.where(erow == i1, row_of, 0.0), axis=0, keepdims=True).astype(jnp.int32)
    p2 = jnp.sum(jnp.where(erow == i2, row_of, 0.0), axis=0, keepdims=True).astype(jnp.int32)

    rowi = lax.broadcasted_iota(jnp.int32, (nr, tb), 0)
    hit1 = rowi == p1
    hit2 = rowi == p2
    pm = jnp.where(hit1, 1.0, jnp.where(hit2, 1.0, 0.0)).astype(BF16)
    cw = 512
    for c0 in range(0, d, cw):
        srt_ref[:, c0:c0 + cw] = jnp.dot(pm, hb[:, c0:c0 + cw], preferred_element_type=F32).astype(BF16)
    wcol = jnp.sum(jnp.where(hit1, w1, jnp.where(hit2, w2, 0.0)), axis=1, keepdims=True)
    rnd = lambda v: v.astype(BF16).astype(F32)
    hi = rnd(wcol)
    mid = rnd(wcol - hi)
    lo = (wcol - hi) - mid
    lane = lax.broadcasted_iota(jnp.int32, (nr, LANES), 1)
    extra = jnp.where(lane == 0, hi, jnp.where(lane == 1, mid, jnp.where(lane == 2, lo, 0.0)))
    srt_ref[:, d:] = extra.astype(BF16)
    prow = lax.broadcasted_iota(jnp.int32, (8, tb), 0)
    pos_ref[...] = jnp.where(prow == 0, p1, jnp.where(prow == 1, p2, 0))
    nch_ref[...] = jnp.broadcast_to(nch, (N_EXPERTS, LANES))


def _route_sort_prompt(x, g, m4, layer, w_grp, b_grp, w_exp, b_exp, seq):
    t, d = x.shape
    nblk = t // MOE_TB
    tpb = seq // MOE_TB
    pad = jnp.zeros((8 - N_GROUPS, d), F32)
    wt = jnp.concatenate([w_grp.T, pad, w_exp.T], axis=0)
    bt = jnp.concatenate([b_grp, jnp.zeros((8 - N_GROUPS,), F32), b_exp]).reshape(MOE_ROUTE_ROWS, 1)
    full = lambda shape: pl.BlockSpec(shape, lambda i: (0,) * len(shape))
    return pl.pallas_call(
        _route_sort_kernel,
        grid=(nblk,),
        in_specs=[pl.BlockSpec((MOE_TB, d), lambda i: (i, 0)), full((1, d)),
                  _mod_spec(layer, 3, tpb, d), _mod_spec(layer, 4, tpb, d),
                  full((MOE_ROUTE_ROWS, d)), full((MOE_ROUTE_ROWS, 1))],
        out_specs=(pl.BlockSpec((MOE_NR, d + LANES), lambda i: (i, 0)),
                   pl.BlockSpec((None, 8, MOE_TB), lambda i: (i, 0, 0)),
                   pl.BlockSpec((None, N_EXPERTS, LANES), lambda i: (i, 0, 0))),
        out_shape=(jax.ShapeDtypeStruct((nblk * MOE_NR, d + LANES), BF16),
                   jax.ShapeDtypeStruct((nblk, 8, MOE_TB), jnp.int32),
                   jax.ShapeDtypeStruct((nblk, N_EXPERTS, LANES), jnp.int32)),
        compiler_params=_cparams(("arbitrary",)),
        name="moe_route_sort",
    )(x, g.reshape(1, d), m4, m4, wt, bt)


def _moe_n_tiles(nblk):
    return (nblk * (MOE_NCH - 1)) // MOE_CPT + N_EXPERTS


def _moe_schedule(nch):
    nblk = nch.shape[0]
    n_tiles = _moe_n_tiles(nblk)
    off_c = jnp.cumsum(nch, axis=1) - nch
    n_e = nch.sum(axis=0)
    tiles_e = (n_e + MOE_CPT - 1) // MOE_CPT
    tile_end = jnp.cumsum(tiles_e)
    tile_start = tile_end - tiles_e
    tau = jnp.arange(n_tiles, dtype=jnp.int32)
    e = jnp.minimum((tau[:, None] >= tile_end[None, :]).sum(axis=1), N_EXPERTS - 1).astype(jnp.int32)
    oe = (e[:, None] == jnp.arange(N_EXPERTS, dtype=jnp.int32)[None, :]).astype(jnp.int32)
    pick_e = lambda table: (oe * table[None, :]).sum(axis=1)
    rows_e = lambda table: (oe[:, :, None] * table.T[None, :, :]).sum(axis=1)
    g = (tau - pick_e(tile_start))[:, None] * MOE_CPT + jnp.arange(MOE_CPT, dtype=jnp.int32)[None, :]
    valid = (tau < tile_end[-1])[:, None] & (g < pick_e(n_e)[:, None])
    nch_e = rows_e(nch)
    cum_b = jnp.cumsum(nch_e, axis=1)
    b = jnp.minimum((g[:, :, None] >= cum_b[:, None, :]).sum(axis=-1), nblk - 1)
    ob = (b[:, :, None] == jnp.arange(nblk, dtype=jnp.int32)[None, None, :]).astype(jnp.int32)
    pick_b = lambda table: (ob * table[:, None, :]).sum(axis=-1)
    chunk = b * MOE_NCH + pick_b(rows_e(off_c)) + (g - pick_b(cum_b - nch_e))
    src = jnp.where(valid, chunk, MOE_NCH - 1)
    spare = nblk * MOE_NCH + (tau % 2)[:, None] * MOE_CPT + jnp.arange(MOE_CPT, dtype=jnp.int32)[None, :]
    dst = jnp.where(valid, chunk, spare)
    return (e, src.reshape(-1).astype(jnp.int32), dst.reshape(-1).astype(jnp.int32),
            tile_end[-1:].astype(jnp.int32))


def _moe_expert_kernel(te_ref, src_ref, dst_ref, nv_ref, srt_hbm, w1_ref, w3_ref, w2_ref, yin_hbm, y_hbm,
                       xbuf, ybuf, w1b, w3b, w2b, sem_in, sem_out, *, n_tiles, d):
    del yin_hbm
    t = pl.program_id(0)
    slot = t % 2

    def in_copy(tile, s, c):
        row = pl.multiple_of(src_ref[tile * MOE_CPT + c] * MOE_CH, MOE_CH)
        return pltpu.make_async_copy(srt_hbm.at[pl.ds(row, MOE_CH), :], xbuf.at[s, pl.ds(c * MOE_CH, MOE_CH), :],
                                     sem_in.at[s])

    def out_copy(tile, s, c):
        row = pl.multiple_of(dst_ref[tile * MOE_CPT + c] * MOE_CH, MOE_CH)
        return pltpu.make_async_copy(ybuf.at[s, pl.ds(c * MOE_CH, MOE_CH), :], y_hbm.at[pl.ds(row, MOE_CH), :],
                                     sem_out.at[s])

    @pl.when(t == 0)
    def _():
        for c in range(MOE_CPT):
            in_copy(0, 0, c).start()

    @pl.when(t + 1 < n_tiles)
    def _():
        for c in range(MOE_CPT):
            in_copy(t + 1, 1 - slot, c).start()

    for c in range(MOE_CPT):
        in_copy(t, slot, c).wait()

    @pl.when(t >= 2)
    def _():
        for c in range(MOE_CPT):
            out_copy(t - 2, slot, c).wait()

    @pl.when((t == 0) | (te_ref[t] != te_ref[jnp.maximum(t - 1, 0)]))
    def _():
        w1b[...] = w1_ref[...].astype(BF16)
        w3b[...] = w3_ref[...].astype(BF16)
        w2b[...] = w2_ref[...].astype(BF16)

    @pl.when(t < nv_ref[0])
    def _():
        x = xbuf[slot]
        h = x[:, :d]
        wcol = jnp.sum(x[:, d:].astype(F32), axis=1, keepdims=True)
        a = jnp.dot(h, w1b[...], preferred_element_type=F32)
        b = jnp.dot(h, w3b[...], preferred_element_type=F32)
        hid = (_silu(a) * b * wcol).astype(BF16)
        ybuf[slot] = jnp.dot(hid, w2b[...], preferred_element_type=F32).astype(BF16)

    for c in range(MOE_CPT):
        out_copy(t, slot, c).start()

    @pl.when(t == n_tiles - 1)
    def _():
        for c in range(MOE_CPT):
            out_copy(t, slot, c).wait()
        if n_tiles >= 2:
            for c in range(MOE_CPT):
                out_copy(t - 1, 1 - slot, c).wait()


def _moe_experts(srt, te, src, dst, n_used, w1, w3, w2, layer, nblk):
    d = srt.shape[1] - LANES
    f = w1.shape[-1]
    n_tiles = _moe_n_tiles(nblk)
    tile_rows = MOE_CPT * MOE_CH
    y_rows = nblk * MOE_NR + 2 * tile_rows
    wspec = lambda shape: pl.BlockSpec((None, None) + shape, lambda t, te, src, dst, nv: (layer, te[t], 0, 0))
    return pl.pallas_call(
        functools.partial(_moe_expert_kernel, n_tiles=n_tiles, d=d),
        grid_spec=pltpu.PrefetchScalarGridSpec(
            num_scalar_prefetch=4,
            grid=(n_tiles,),
            in_specs=[pl.BlockSpec(memory_space=pl.ANY), wspec((d, f)), wspec((d, f)), wspec((f, d)),
                      pl.BlockSpec(memory_space=pl.ANY)],
            out_specs=pl.BlockSpec(memory_space=pl.ANY),
            scratch_shapes=[pltpu.VMEM((2, tile_rows, d + LANES), BF16), pltpu.VMEM((2, tile_rows, d), BF16),
                            pltpu.VMEM((d, f), BF16), pltpu.VMEM((d, f), BF16), pltpu.VMEM((f, d), BF16),
                            pltpu.SemaphoreType.DMA((2,)), pltpu.SemaphoreType.DMA((2,))]),
        out_shape=jax.ShapeDtypeStruct((y_rows, d), BF16),
        input_output_aliases={8: 0},
        compiler_params=_cparams(("arbitrary",)),
        name="moe_experts",
    )(te, src, dst, n_used, srt, w1, w3, w2, jnp.zeros((y_rows, d), BF16))


def _unsort_kernel(*refs, last):
    if last:
        y_ref, pos_ref, x_ref, gate_ref, g_ref, out_ref, xn_ref = refs
    else:
        y_ref, pos_ref, x_ref, gate_ref, g_ref, sh_ref, sc_ref, xn_ref, h_ref = refs
    nr = y_ref.shape[0]
    tb, d = x_ref.shape
    rowi = lax.broadcasted_iota(jnp.int32, (nr, tb), 0)
    pm = jnp.where(rowi == pos_ref[0:1, :], 1.0, jnp.where(rowi == pos_ref[1:2, :], 1.0, 0.0)).astype(BF16)
    cw = 512
    for c0 in range(0, d, cw):
        y = lax.dot_general(pm, y_ref[:, c0:c0 + cw], (((0,), (0,)), ((), ())), preferred_element_type=F32)
        xn_ref[:, c0:c0 + cw] = x_ref[:, c0:c0 + cw] + gate_ref[:, c0:c0 + cw] * y
    normed = _rmsnorm(xn_ref[...], g_ref[...])
    if last:
        out_ref[...] = normed
    else:
        h_ref[...] = (normed * (1.0 + sc_ref[...]) + sh_ref[...]).astype(h_ref.dtype)


def _moe_unsort(y_srt, pos, x, m4, layer, seq, g_next, last):
    t, d = x.shape
    tpb = seq // MOE_TB
    rows = pl.BlockSpec((MOE_TB, d), lambda i: (i, 0))
    ins = [y_srt, pos, x, m4, g_next.reshape(1, d)]
    in_specs = [pl.BlockSpec((MOE_NR, d), lambda i: (i, 0)), pl.BlockSpec((None, 8, MOE_TB), lambda i: (i, 0, 0)),
                rows, _mod_spec(layer, 5, tpb, d), pl.BlockSpec((1, d), lambda i: (0, 0))]
    if last:
        out_shape, out_specs = jax.ShapeDtypeStruct((t, d), F32), rows
        scratch = [pltpu.VMEM((MOE_TB, d), F32)]
    else:
        ins += [m4, m4]
        in_specs += [_mod_spec(layer + 1, 0, tpb, d), _mod_spec(layer + 1, 1, tpb, d)]
        out_shape = (jax.ShapeDtypeStruct((t, d), F32), jax.ShapeDtypeStruct((t, d), BF16))
        out_specs, scratch = (rows, rows), []
    return pl.pallas_call(
        functools.partial(_unsort_kernel, last=last),
        grid=(t // MOE_TB,),
        in_specs=in_specs, out_specs=out_specs, out_shape=out_shape, scratch_shapes=scratch,
        compiler_params=_cparams(("arbitrary",)),
        name="moe_unsort",
    )(*ins)


def _s_in_kernel(*refs, act, has_bias):
    if has_bias:
        x_ref, g_ref, sh_ref, sc_ref, w_ref, b_ref, o_ref = refs
    else:
        x_ref, g_ref, sh_ref, sc_ref, w_ref, o_ref = refs
    h = _rmsnorm(x_ref[...], g_ref[...]) * (1.0 + sc_ref[...]) + sh_ref[...]
    y = _dot_lp(h, w_ref[...])
    if has_bias:
        y = y + b_ref[...]
    if act == "gelu":
        y = _gelu_tanh(y)
    elif act == "logsig":
        y = _log_sigmoid(y)
    o_ref[...] = y


def _s_in(x, g, sh, sc, w, bias, act, tn):
    m, d = x.shape
    n = w.shape[1]
    tn = min(tn, n)
    full = pl.BlockSpec((m, d), lambda j: (0, 0))
    ins = [x, g.reshape(1, d), sh, sc, w]
    in_specs = [full, pl.BlockSpec((1, d), lambda j: (0, 0)), full, full, pl.BlockSpec((d, tn), lambda j: (0, j))]
    if bias is not None:
        ins.append(bias.reshape(1, n))
        in_specs.append(pl.BlockSpec((1, tn), lambda j: (0, j)))
    return pl.pallas_call(
        functools.partial(_s_in_kernel, act=act, has_bias=bias is not None),
        grid=(n // tn,),
        in_specs=in_specs,
        out_specs=pl.BlockSpec((m, tn), lambda j: (0, j)),
        out_shape=jax.ShapeDtypeStruct((m, n), F32),
        compiler_params=_cparams(("arbitrary",)),
        name="s_in_" + act,
    )(*ins)


def _s_out_kernel(*refs, has_bias):
    if has_bias:
        a_ref, w_ref, b_ref, x_ref, gate_ref, o_ref = refs
    else:
        a_ref, w_ref, x_ref, gate_ref, o_ref = refs
    y = _dot_lp(a_ref[...], w_ref[...])
    if has_bias:
        y = y + b_ref[...]
    o_ref[...] = x_ref[...] + gate_ref[...] * y


def _s_out(a, w, bias, x, gate, tn=512):
    m, k = a.shape
    n = w.shape[1]
    col = pl.BlockSpec((m, tn), lambda j: (0, j))
    ins = [a, w]
    in_specs = [pl.BlockSpec((m, k), lambda j: (0, 0)), pl.BlockSpec((k, tn), lambda j: (0, j))]
    if bias is not None:
        ins.append(bias.reshape(1, n))
        in_specs.append(pl.BlockSpec((1, tn), lambda j: (0, j)))
    ins += [x, gate]
    in_specs += [col, col]
    return pl.pallas_call(
        functools.partial(_s_out_kernel, has_bias=bias is not None),
        grid=(n // tn,),
        in_specs=in_specs, out_specs=col,
        out_shape=jax.ShapeDtypeStruct((m, n), F32),
        compiler_params=_cparams(("arbitrary",)),
        name="s_out",
    )(*ins)


def _s_sgu_kernel(z_ref, lng_ref, lnb_ref, w00_ref, b0_ref, gv_ref, o_ref, *, width):
    u = z_ref[:, :width]
    v = z_ref[:, width:]
    mu = jnp.mean(v, axis=-1, keepdims=True)
    vc = v - mu
    vn = vc * lax.rsqrt(jnp.mean(vc * vc, axis=-1, keepdims=True) + LN_EPS) * lng_ref[...] + lnb_ref[...]
    gv_ref[...] = vn
    o_ref[...] = u * (vn * w00_ref[...] + b0_ref[...])


def _s_sgu(z, ln_g, ln_b, ws, bs):
    m, n2 = z.shape
    width = n2 // 2
    gd = width // GM_GROUPS
    w00 = jnp.repeat(ws[:, 0, 0], gd).reshape(1, width)
    b0 = jnp.repeat(bs[:, 0], gd).reshape(1, width)
    vec = pl.BlockSpec((1, width), lambda: (0, 0))
    blk = pl.BlockSpec((m, width), lambda: (0, 0))
    return pl.pallas_call(
        functools.partial(_s_sgu_kernel, width=width),
        in_specs=[pl.BlockSpec((m, n2), lambda: (0, 0)), vec, vec, vec, vec],
        out_specs=(blk, blk),
        out_shape=(jax.ShapeDtypeStruct((m, width), F32), jax.ShapeDtypeStruct((m, width), F32)),
        name="s_sgu",
    )(z, ln_g.reshape(1, width), ln_b.reshape(1, width), w00, b0)


def _s_attn_kernel(pt_ref, q_ref, kn_ref, vn_ref, lfn_ref, cnf_ref, *refs, n_steps, pps, page, nh, scale):
    del pt_ref
    ck_refs, cv_refs, clf_refs = refs[:pps], refs[pps:2 * pps], refs[2 * pps:3 * pps]
    o_ref, m_ref, l_ref, acc_ref, carry_ref = refs[3 * pps:]
    p = pl.program_id(1)
    pflat = page * nh
    flat = pps * pflat
    lanes = m_ref.shape[1]

    @pl.when(p == 0)
    def _():
        m_ref[...] = jnp.full_like(m_ref, NEG_INF)
        l_ref[...] = jnp.zeros_like(l_ref)
        acc_ref[...] = jnp.zeros_like(acc_ref)
        carry_ref[...] = jnp.zeros_like(carry_ref)

    def per_head(x, op):
        y = x[:, :lanes]
        for i in range(1, flat // lanes):
            y = op(y, x[:, i * lanes:(i + 1) * lanes])
        s = nh
        while s < lanes:
            y = op(y, pltpu.roll(y, s, axis=1))
            s *= 2
        return y

    tile = lambda y: jnp.concatenate([y] * (flat // lanes), axis=1)

    def to_col(row):
        r = lax.broadcasted_iota(jnp.int32, (nh, nh), 0)
        c = lax.broadcasted_iota(jnp.int32, (nh, nh), 1)
        return jnp.sum(jnp.where(r == c, jnp.broadcast_to(row, (nh, nh)), 0.0), axis=1, keepdims=True)

    qb = q_ref[...].astype(BF16)
    own_head = (lax.broadcasted_iota(jnp.int32, (nh, pflat), 1) % nh) == lax.broadcasted_iota(jnp.int32, (nh, pflat), 0)

    lf = jnp.concatenate([r[...] for r in clf_refs], axis=1)
    lane_idx = lax.broadcasted_iota(jnp.int32, (1, flat), 1)
    inc = lf
    step = nh
    while step < flat:
        if step % lanes == 0:
            shifted = jnp.concatenate([inc[:, step:], jnp.zeros((1, step), F32)], axis=1)
        else:
            shifted = jnp.where(lane_idx < flat - step, pltpu.roll(inc, flat - step, axis=1), 0.0)
        inc = inc + shifted
        step *= 2
    rest = (inc - lf) + tile(carry_ref[...])
    carry_ref[...] = carry_ref[...] + per_head(lf, jnp.add)

    def own_logits(k_ref):
        sf = lax.dot_general(qb, k_ref[...].astype(BF16), (((1,), (1,)), ((), ())), preferred_element_type=F32)
        return jnp.sum(jnp.where(own_head, sf, 0.0), axis=0, keepdims=True)

    qk = jnp.concatenate([own_logits(r) for r in ck_refs], axis=1)
    s = qk * scale + jnp.concatenate([cnf_ref[...]] * pps, axis=1) + rest
    m_old = m_ref[...]
    m_new = jnp.maximum(m_old, per_head(s, jnp.maximum))
    alpha = jnp.exp(m_old - m_new)
    pr = jnp.exp(s - tile(m_new))
    l_ref[...] = l_ref[...] * alpha + per_head(pr, jnp.add)
    m_ref[...] = m_new
    pv = None
    for j, v_ref in enumerate(cv_refs):
        pr_j = pr[:, j * pflat:(j + 1) * pflat]
        pm = jnp.where(own_head, jnp.broadcast_to(pr_j, (nh, pflat)), 0.0).astype(BF16)
        d = jnp.dot(pm, v_ref[...].astype(BF16), preferred_element_type=F32)
        pv = d if pv is None else pv + d
    acc_ref[...] = acc_ref[...] * to_col(alpha[:, :nh]) + pv

    @pl.when(p == n_steps - 1)
    def _():
        rnd = lambda x: x.astype(BF16).astype(F32)
        c_new = lfn_ref[...]
        sn = lax.dot_general(qb, kn_ref[...].astype(BF16), (((1,), (1,)), ((), ())), preferred_element_type=F32)
        r = lax.broadcasted_iota(jnp.int32, (nh, nh), 0)
        c = lax.broadcasted_iota(jnp.int32, (nh, nh), 1)
        s_n = jnp.sum(jnp.where(r == c, sn, 0.0), axis=0, keepdims=True) * scale + (c_new - c_new)
        m_o = m_ref[:, :nh]
        m_n = jnp.maximum(m_o, s_n)
        alpha_n = jnp.exp(m_o - m_n)
        pr_n = jnp.exp(s_n - m_n)
        l_n = l_ref[:, :nh] * alpha_n + pr_n
        acc = acc_ref[...] * to_col(alpha_n) + to_col(rnd(pr_n)) * rnd(vn_ref[...])
        o_ref[...] = acc / to_col(l_n)


def _s_attn(q, k_new, v_new, lf_new, cache_k, cache_v, cache_logf, page_table, slot):
    bsz, d = q.shape
    n_fox, n_pool, page, nh, hd = cache_k.shape
    n_pages = page_table.shape[1]
    flat = page * nh
    pps = next(c for c in (4, 2, 1) if n_pages % c == 0)
    n_steps = n_pages // pps
    heads = pl.BlockSpec((None, nh, hd), lambda b, p, pt: (b, 0, 0))
    per_seq = lambda n: pl.BlockSpec((None, 1, n), lambda b, p, pt: (b, 0, 0))

    def pg_idx(j):
        return lambda b, p, pt: (slot, pt[b * n_pages + (n_steps - 1 - p) * pps + j], 0, 0)

    cache_k = cache_k.reshape(n_fox, n_pool, flat, hd)
    cache_v = cache_v.reshape(n_fox, n_pool, flat, hd)
    clf_flat = cache_logf.reshape(n_fox, n_pool, 1, flat)
    cn_flat = jnp.tile(lf_new, (1, page)).reshape(bsz, 1, flat)
    kv_specs = [pl.BlockSpec((None, None, flat, hd), pg_idx(j)) for j in range(pps)]
    lf_specs = [pl.BlockSpec((None, None, 1, flat), pg_idx(j)) for j in range(pps)]
    out = pl.pallas_call(
        functools.partial(_s_attn_kernel, n_steps=n_steps, pps=pps, page=page, nh=nh, scale=hd ** -0.5),
        grid_spec=pltpu.PrefetchScalarGridSpec(
            num_scalar_prefetch=1,
            grid=(bsz, n_steps),
            in_specs=[heads, heads, heads, per_seq(nh), per_seq(flat)] + kv_specs + kv_specs + lf_specs,
            out_specs=heads,
            scratch_shapes=[pltpu.VMEM((1, LANES), F32), pltpu.VMEM((1, LANES), F32), pltpu.VMEM((nh, hd), F32),
                            pltpu.VMEM((1, LANES), F32)]),
        out_shape=jax.ShapeDtypeStruct((bsz, nh, hd), F32),
        compiler_params=_cparams(("arbitrary", "arbitrary")),
        name="s_attn",
    )(page_table.reshape(-1), q.reshape(bsz, nh, hd), k_new.reshape(bsz, nh, hd), v_new.reshape(bsz, nh, hd),
      lf_new.reshape(bsz, 1, nh), cn_flat, *([cache_k] * pps), *([cache_v] * pps), *([clf_flat] * pps))
    return out.reshape(bsz, d)


def _s_moe_kernel(eid_ref, h_ref, wts_ref, w1_ref, w3_ref, w2_ref, x_ref, gate_ref, o_ref, acc_ref, *, n_pairs):
    del eid_ref
    p = pl.program_id(0)

    @pl.when(p == 0)
    def _():
        acc_ref[...] = jnp.zeros_like(acc_ref)

    h = h_ref[...]
    m = h.shape[0]
    a = _dot_lp(h, w1_ref[...])
    b = _dot_lp(h, w3_ref[...])
    wts = wts_ref[...]
    wcol = jnp.where(p % 2 == 0, wts[:, 0:1], wts[:, 1:2])
    wcol = jnp.where(lax.broadcasted_iota(jnp.int32, (m, 1), 0) == p // 2, wcol, 0.0)
    acc_ref[...] += _dot_lp(_silu(a) * b * wcol, w2_ref[...])

    @pl.when(p == n_pairs - 1)
    def _():
        o_ref[...] = x_ref[...] + gate_ref[...] * acc_ref[...]


def _s_moe(h, ids, wts, w1, w3, w2, x, gate, layer):
    m, d = h.shape
    f = w1.shape[-1]
    n_pairs = 2 * m
    full = lambda shape: pl.BlockSpec(shape, lambda p, eid: (0,) * len(shape))
    return pl.pallas_call(
        functools.partial(_s_moe_kernel, n_pairs=n_pairs),
        grid_spec=pltpu.PrefetchScalarGridSpec(
            num_scalar_prefetch=1,
            grid=(n_pairs,),
            in_specs=[full((m, d)), full((m, 2)),
                      pl.BlockSpec((None, None, d, f), lambda p, eid: (layer, eid[p], 0, 0)),
                      pl.BlockSpec((None, None, d, f), lambda p, eid: (layer, eid[p], 0, 0)),
                      pl.BlockSpec((None, None, f, d), lambda p, eid: (layer, eid[p], 0, 0)),
                      full((m, d)), full((m, d))],
            out_specs=full((m, d)),
            scratch_shapes=[pltpu.VMEM((m, d), F32)]),
        out_shape=jax.ShapeDtypeStruct((m, d), F32),
        compiler_params=_cparams(("arbitrary",)),
        name="s_moe",
    )(ids.reshape(-1), h, wts, w1, w3, w2, x, gate)


def kernel(x_prompt, x_sample, cache_k, cache_v, cache_logf, page_table, c_prompt, c_sample,
           w_ada, b_ada, norm1_g, norm2_g, final_g,
           gm_w_in, gm_b_in, gm_ln_g, gm_ln_b, gm_ws, gm_bs, gm_w_out, gm_b_out,
           fox_w_qkv, fox_w_f, fox_b_f, fox_w_o,
           moe_w_grp, moe_b_grp, moe_w_exp, moe_b_exp, moe_w1, moe_w3, moe_w2):
    bsz, seq, d = x_prompt.shape
    dbsz = x_sample.shape[0]
    depth = w_ada.shape[0]
    nh, hd = cache_k.shape[3], cache_k.shape[4]
    t = bsz * seq
    assert bsz + dbsz <= ADA_ROWS and x_sample.shape[1] == 1

    c16 = jnp.concatenate([c_prompt, c_sample, jnp.zeros((ADA_ROWS - bsz - dbsz, d), F32)], axis=0)
    m_all = _ada_all(c16, w_ada, b_ada)
    m4 = m_all.reshape(depth, ADA_ROWS, 1, 6 * d)

    xp = x_prompt.reshape(t, d)
    xs = x_sample.reshape(dbsz, d)
    k_p, v_p, lf_p, k_s, v_s, lf_s, gv_s = [], [], [], [], [], [], []
    hp = _norm_mod_prompt(xp, norm1_g[0], m4, 0, 0, seq)
    y_prompt = None

    for i in range(depth):
        slot = i // 2
        ms = m_all[i, bsz:bsz + dbsz].reshape(dbsz, 6, d)
        if i % 2 == 0:
            z = _mm_prompt("gelu", hp, gm_w_in[slot], bias=gm_b_in[slot])
            gated = _sgu_prompt(z, gm_ln_g[slot], gm_ln_b[slot], gm_ws[slot], gm_bs[slot])
            xp = _mm_prompt("resid", gated, gm_w_out[slot], bias=gm_b_out[slot], x=xp, m4=m4, layer=i, gate_slot=2,
                            seq=seq)
            zs = _s_in(xs, norm1_g[i], ms[:, 0], ms[:, 1], gm_w_in[slot], gm_b_in[slot], "gelu", 512)
            gv, gs = _s_sgu(zs, gm_ln_g[slot], gm_ln_b[slot], gm_ws[slot], gm_bs[slot])
            gv_s.append(gv)
            xs = _s_out(gs, gm_w_out[slot], gm_b_out[slot], xs, ms[:, 2])
        else:
            wqkv = fox_w_qkv[slot]
            q = _mm_prompt("proj_bf16", hp, wqkv, col0=0, n=d, out_scale=hd ** -0.5)
            kf, kb = _mm_prompt("proj_both", hp, wqkv, col0=d, n=d)
            vf, vb = _mm_prompt("proj_both", hp, wqkv, col0=2 * d, n=d)
            lf, qa, ka = _logf_prompt(hp, fox_w_f[slot], fox_b_f[slot], bsz, seq, hd)
            b3 = lambda a: a.reshape(bsz, seq, d)
            o = _fox_attn_prompt(b3(q), b3(qa), b3(kb), b3(ka), b3(vb), bsz, seq, nh, hd)
            xp = _mm_prompt("resid", o.reshape(t, d), fox_w_o[slot], bias=jnp.zeros((d,), F32), x=xp, m4=m4, layer=i,
                            gate_slot=2, seq=seq)
            k_p.append(kf.reshape(bsz, seq, nh, hd))
            v_p.append(vf.reshape(bsz, seq, nh, hd))
            lf_p.append(lf.reshape(bsz, seq, nh))
            qkv = _s_in(xs, norm1_g[i], ms[:, 0], ms[:, 1], wqkv, None, "none", 512)
            lfn = _s_in(xs, norm1_g[i], ms[:, 0], ms[:, 1], fox_w_f[slot], fox_b_f[slot], "logsig", 512)
            qs, kn, vn = qkv[:, :d], qkv[:, d:2 * d], qkv[:, 2 * d:]
            o_s = _s_attn(qs, kn, vn, lfn, cache_k, cache_v, cache_logf, page_table, slot)
            xs = _s_out(o_s, fox_w_o[slot], None, xs, ms[:, 2])
            k_s.append(kn.reshape(dbsz, 1, nh, hd))
            v_s.append(vn.reshape(dbsz, 1, nh, hd))
            lf_s.append(lfn.reshape(dbsz, 1, nh))

        srt, pos, nch = _route_sort_prompt(xp, norm2_g[i], m4, i, moe_w_grp[i], moe_b_grp[i], moe_w_exp[i],
                                           moe_b_exp[i], seq)
        te, src, dst, n_used = _moe_schedule(nch[:, :, 0])
        y_srt = _moe_experts(srt, te, src, dst, n_used, moe_w1, moe_w3, moe_w2, i, t // MOE_TB)
        if i + 1 < depth:
            xp, hp = _moe_unsort(y_srt, pos, xp, m4, i, seq, norm1_g[i + 1], last=False)
        else:
            y_prompt = _moe_unsort(y_srt, pos, xp, m4, i, seq, final_g, last=True).reshape(bsz, seq, d)
        full8 = pl.BlockSpec((dbsz, d), lambda j: (0, 0))
        h2s, _, ids, wts = _router(xs, norm2_g[i], full8, full8, (ms[:, 3], ms[:, 4]), moe_w_grp[i], moe_b_grp[i],
                                   moe_w_exp[i], moe_b_exp[i], dbsz, F32)
        xs = _s_moe(h2s, ids, wts, moe_w1, moe_w3, moe_w2, xs, ms[:, 5], i)

    y_sample = _final_norm(xs, final_g, dbsz).reshape(dbsz, 1, d)
    return (y_prompt, y_sample, jnp.stack(k_p), jnp.stack(v_p), jnp.stack(lf_p),
            jnp.stack(k_s), jnp.stack(v_s), jnp.stack(lf_s), jnp.stack(gv_s).reshape(len(gv_s), dbsz, 1, d))
```

```python
import functools
import math

import jax
import jax.numpy as jnp
from jax import lax
from jax.experimental import pallas as pl
from jax.experimental.pallas import tpu as pltpu

F32 = jnp.float32
BF16 = jnp.bfloat16

RMS_EPS = 1e-6
LN_EPS = 1e-5
NEG_INF = -1e30
CHUNK = 128
GM_GROUPS = 16
N_GROUPS = 4
EXP_PER_GROUP = 4
N_EXPERTS = N_GROUPS * EXP_PER_GROUP
LANES = 128
BF16_ROWS = 16
ADA_ROWS = 16

V7X_VMEM_LIMIT = 56 * 1024 * 1024


def _cparams(sem, vmem=V7X_VMEM_LIMIT):
    return pltpu.CompilerParams(dimension_semantics=sem, vmem_limit_bytes=vmem)


def _silu(x):
    return x / (1.0 + jnp.exp(-x))


def _gelu_tanh(x):
    return 0.5 * x * (1.0 + jnp.tanh(math.sqrt(2.0 / math.pi) * (x + 0.044715 * (x * x * x))))


def _log_sigmoid(x):
    y = -x
    return -(jnp.maximum(y, 0.0) + jnp.log1p(jnp.exp(-jnp.abs(y))))


def _rmsnorm(x, g):
    return x * lax.rsqrt(jnp.mean(x * x, axis=-1, keepdims=True) + RMS_EPS) * g


def _stack_split(a, terms):
    parts = []
    r = a
    for _ in range(terms):
        p = r.astype(BF16).astype(F32)
        parts.append(p)
        r = r - p
    return jnp.concatenate(parts, axis=0).astype(BF16)


def _fold_rows(r, m, terms):
    out = r[:m]
    for t in range(1, terms):
        out = out + r[t * m:(t + 1) * m]
    return out


def _dot_lp(a, b):
    m = a.shape[0]
    pad = (-m) % BF16_ROWS
    if pad:
        a = jnp.concatenate([a, jnp.zeros((pad, a.shape[1]), F32)], axis=0)
    r = jnp.dot(a.astype(BF16), b.astype(BF16), preferred_element_type=F32)
    return r[:m] if pad else r


def _dot_split_exact_rhs(a, b_bf16, terms=2):
    m = a.shape[0]
    mp = m + (-m) % BF16_ROWS
    if mp != m:
        a = jnp.concatenate([a, jnp.zeros((mp - m, a.shape[1]), F32)], axis=0)
    r = jnp.dot(_stack_split(a, terms), b_bf16, preferred_element_type=F32)
    return _fold_rows(r, mp, terms)[:m]


def _dot_exact_lhs_split(a_bf16, b, terms=3):
    out = None
    r = b
    for _ in range(terms):
        p = r.astype(BF16)
        d = jnp.dot(a_bf16, p, preferred_element_type=F32)
        out = d if out is None else out + d
        r = r - p.astype(F32)
    return out


def _ada_kernel(c_ref, w_ref, b_ref, o_ref):
    o_ref[...] = _dot_lp(_silu(c_ref[...]), w_ref[...]) + b_ref[...]


def _ada_all(c16, w_ada, b_ada, tn=1024):
    depth, d, n6 = w_ada.shape
    return pl.pallas_call(
        _ada_kernel,
        grid=(depth, n6 // tn),
        in_specs=[pl.BlockSpec((ADA_ROWS, d), lambda l, j: (0, 0)),
                  pl.BlockSpec((None, d, tn), lambda l, j: (l, 0, j)),
                  pl.BlockSpec((None, 1, tn), lambda l, j: (l, 0, j))],
        out_specs=pl.BlockSpec((None, ADA_ROWS, tn), lambda l, j: (l, 0, j)),
        out_shape=jax.ShapeDtypeStruct((depth, ADA_ROWS, n6), F32),
        compiler_params=_cparams(("arbitrary", "arbitrary")),
        name="ada_all",
    )(c16, w_ada, b_ada.reshape(depth, 1, n6))


def _norm_mod_kernel(x_ref, g_ref, sh_ref, sc_ref, o_ref):
    y = _rmsnorm(x_ref[...], g_ref[...])
    o_ref[...] = (y * (1.0 + sc_ref[...]) + sh_ref[...]).astype(o_ref.dtype)


def _norm_kernel(x_ref, g_ref, o_ref):
    o_ref[...] = _rmsnorm(x_ref[...], g_ref[...]).astype(o_ref.dtype)


def _mod_spec(layer, slot, tiles_per_batch, d):
    return pl.BlockSpec((None, None, 1, d), lambda i, *_: (layer, i // tiles_per_batch, 0, slot))


def _norm_mod_prompt(x, g, m4, layer, slot_shift, seq, tm=512):
    t, d = x.shape
    tpb = seq // tm
    return pl.pallas_call(
        _norm_mod_kernel,
        grid=(t // tm,),
        in_specs=[pl.BlockSpec((tm, d), lambda i: (i, 0)),
                  pl.BlockSpec((1, d), lambda i: (0, 0)),
                  _mod_spec(layer, slot_shift, tpb, d),
                  _mod_spec(layer, slot_shift + 1, tpb, d)],
        out_specs=pl.BlockSpec((tm, d), lambda i: (i, 0)),
        out_shape=jax.ShapeDtypeStruct((t, d), BF16),
        compiler_params=_cparams(("arbitrary",)),
        name="norm_mod_prompt",
    )(x, g.reshape(1, d), m4, m4)


def _final_norm(x, g, tm):
    t, d = x.shape
    return pl.pallas_call(
        _norm_kernel,
        grid=(t // tm,),
        in_specs=[pl.BlockSpec((tm, d), lambda i: (i, 0)), pl.BlockSpec((1, d), lambda i: (0, 0))],
        out_specs=pl.BlockSpec((tm, d), lambda i: (i, 0)),
        out_shape=jax.ShapeDtypeStruct((t, d), F32),
        compiler_params=_cparams(("arbitrary",)),
        name="final_norm",
    )(x, g.reshape(1, d))


def _mm_kernel(*refs, kind, out_scale):
    if kind == "gelu":
        a_ref, w_ref, b_ref, z_ref, wb_ref = refs
    elif kind == "resid":
        a_ref, w_ref, b_ref, x_ref, gate_ref, xo_ref, wb_ref = refs
    elif kind == "proj_bf16":
        a_ref, w_ref, ob_ref, wb_ref = refs
    else:
        a_ref, w_ref, of_ref, ob_ref, wb_ref = refs

    @pl.when(pl.program_id(1) == 0)
    def _():
        wb_ref[...] = w_ref[...].astype(BF16)

    acc = jnp.dot(a_ref[...], wb_ref[...], preferred_element_type=F32)
    if kind == "gelu":
        z_ref[...] = _gelu_tanh(acc + b_ref[...]).astype(z_ref.dtype)
    elif kind == "resid":
        xo_ref[...] = x_ref[...] + gate_ref[...] * (acc + b_ref[...])
    elif kind == "proj_bf16":
        ob_ref[...] = (acc * out_scale).astype(BF16)
    else:
        of_ref[...] = acc
        ob_ref[...] = acc.astype(BF16)


def _mm_prompt(kind, a, w, *, col0=0, n=None, bias=None, x=None, m4=None, layer=None, gate_slot=None, seq=None,
               out_scale=1.0,
               tm=512, tn=1024):
    t, k = a.shape
    n = w.shape[1] if n is None else n
    jb = col0 // tn
    a_spec = pl.BlockSpec((tm, k), lambda j, i: (i, 0))
    w_spec = pl.BlockSpec((k, tn), lambda j, i: (0, j + jb))
    o_spec = pl.BlockSpec((tm, tn), lambda j, i: (i, j))
    ins, in_specs = [a, w], [a_spec, w_spec]
    if kind in ("gelu", "resid"):
        ins.append(bias.reshape(1, n))
        in_specs.append(pl.BlockSpec((1, tn), lambda j, i: (0, j)))
    if kind == "resid":
        tpb = seq // tm
        ins += [x, m4]
        in_specs += [o_spec, pl.BlockSpec((None, None, 1, tn),
                                          lambda j, i: (layer, i // tpb, 0, gate_slot * (n // tn) + j))]
    if kind == "gelu":
        out_shape, out_specs = jax.ShapeDtypeStruct((t, n), BF16), o_spec
    elif kind == "resid":
        out_shape, out_specs = jax.ShapeDtypeStruct((t, n), F32), o_spec
    elif kind == "proj_bf16":
        out_shape, out_specs = jax.ShapeDtypeStruct((t, n), BF16), o_spec
    else:
        out_shape = (jax.ShapeDtypeStruct((t, n), F32), jax.ShapeDtypeStruct((t, n), BF16))
        out_specs = (o_spec, o_spec)
    return pl.pallas_call(
        functools.partial(_mm_kernel, kind=kind, out_scale=out_scale),
        grid=(n // tn, t // tm),
        in_specs=in_specs, out_specs=out_specs, out_shape=out_shape,
        scratch_shapes=[pltpu.VMEM((k, tn), BF16)],
        compiler_params=_cparams(("arbitrary", "arbitrary")),
        name="mm_" + kind,
    )(*ins)


def _sgu_kernel(z_ref, lng_ref, lnb_ref, ws_ref, bst_ref, o_ref, wt_ref, *, width):
    @pl.when(pl.program_id(0) == 0)
    def _():
        r = lax.broadcasted_iota(jnp.int32, (CHUNK, CHUNK), 0)
        c = lax.broadcasted_iota(jnp.int32, (CHUNK, CHUNK), 1)
        for g in range(GM_GROUPS):
            wt_ref[g] = jnp.where(c <= r, ws_ref[g], 0.0).astype(BF16)

    v = z_ref[:, width:].astype(F32)
    mu = jnp.mean(v, axis=-1, keepdims=True)
    vc = v - mu
    vn = vc * lax.rsqrt(jnp.mean(vc * vc, axis=-1, keepdims=True) + LN_EPS) * lng_ref[...] + lnb_ref[...]
    vb = vn.astype(BF16)
    gd = width // GM_GROUPS
    for g in range(GM_GROUPS):
        lo, hi = g * gd, (g + 1) * gd
        s = jnp.dot(wt_ref[g], vb[:, lo:hi], preferred_element_type=F32) + bst_ref[:, g:g + 1]
        o_ref[:, lo:hi] = (z_ref[:, lo:hi].astype(F32) * s).astype(BF16)


def _sgu_prompt(z, ln_g, ln_b, ws, bs):
    t, n2 = z.shape
    width = n2 // 2
    return pl.pallas_call(
        functools.partial(_sgu_kernel, width=width),
        grid=(t // CHUNK,),
        in_specs=[pl.BlockSpec((CHUNK, n2), lambda i: (i, 0)),
                  pl.BlockSpec((1, width), lambda i: (0, 0)),
                  pl.BlockSpec((1, width), lambda i: (0, 0)),
                  pl.BlockSpec((GM_GROUPS, CHUNK, CHUNK), lambda i: (0, 0, 0)),
                  pl.BlockSpec((CHUNK, GM_GROUPS), lambda i: (0, 0))],
        out_specs=pl.BlockSpec((CHUNK, width), lambda i: (i, 0)),
        out_shape=jax.ShapeDtypeStruct((t, width), BF16),
        scratch_shapes=[pltpu.VMEM((GM_GROUPS, CHUNK, CHUNK), BF16)],
        compiler_params=_cparams(("arbitrary",)),
        name="sgu_prompt",
    )(z, ln_g.reshape(1, width), ln_b.reshape(1, width), ws, bs.T)


def _logf_kernel(h_ref, wf_ref, bf_ref, lf_ref, qa_ref, ka_ref, carry_ref, *, tm, hd):
    @pl.when(pl.program_id(1) == 0)
    def _():
        carry_ref[...] = jnp.zeros_like(carry_ref)

    logit = jnp.dot(h_ref[...], wf_ref[...].astype(BF16), preferred_element_type=F32) + bf_ref[...]
    lf = _log_sigmoid(logit)
    lf_ref[...] = lf
    r = lax.broadcasted_iota(jnp.int32, (tm, tm), 0)
    c = lax.broadcasted_iota(jnp.int32, (tm, tm), 1)
    tri = jnp.where(c <= r, 1.0, 0.0).astype(BF16)
    cum = _dot_exact_lhs_split(tri, lf, 3) + carry_ref[...]
    carry_ref[...] = cum[tm - 1:tm, :]
    lane = lax.broadcasted_iota(jnp.int32, (tm, hd), 1)
    rnd = lambda x: x.astype(BF16).astype(F32)
    for h in range(lf.shape[1]):
        c_h = cum[:, h:h + 1]
        hi = rnd(c_h)
        mid = rnd(c_h - hi)
        lo = (c_h - hi) - mid
        qa = jnp.where(lane == 0, hi, jnp.where(lane == 1, mid, jnp.where(lane == 2, lo, jnp.where(lane < 6, 1.0, 0.0))))
        ka = jnp.where(lane < 3, 1.0, jnp.where(lane == 3, -hi, jnp.where(lane == 4, -mid, jnp.where(lane == 5, -lo, 0.0))))
        qa_ref[:, h * hd:(h + 1) * hd] = qa.astype(BF16)
        ka_ref[:, h * hd:(h + 1) * hd] = ka.astype(BF16)


def _logf_prompt(h, w_f, b_f, bsz, seq, hd, tm=512):
    t, d = h.shape
    nh = w_f.shape[1]
    tpb = seq // tm
    row = lambda n: pl.BlockSpec((tm, n), lambda b, i: (b * tpb + i, 0))
    return pl.pallas_call(
        functools.partial(_logf_kernel, tm=tm, hd=hd),
        grid=(bsz, tpb),
        in_specs=[row(d), pl.BlockSpec((d, nh), lambda b, i: (0, 0)), pl.BlockSpec((1, nh), lambda b, i: (0, 0))],
        out_specs=(row(nh), row(nh * hd), row(nh * hd)),
        out_shape=(jax.ShapeDtypeStruct((t, nh), F32), jax.ShapeDtypeStruct((t, nh * hd), BF16),
                   jax.ShapeDtypeStruct((t, nh * hd), BF16)),
        scratch_shapes=[pltpu.VMEM((1, nh), F32)],
        compiler_params=_cparams(("arbitrary", "arbitrary")),
        name="logf_prompt",
    )(h, w_f, b_f.reshape(1, nh))


def _fox_attn_kernel(q_ref, qa_ref, k_ref, ka_ref, v_ref, o_ref, *, tq, hd):
    qi = pl.program_id(2)
    hps = q_ref.shape[1] // hd
    qts = [jnp.concatenate([q_ref[:, g * hd:(g + 1) * hd], qa_ref[:, g * hd:(g + 1) * hd]], axis=1) for g in range(hps)]
    ones_col = jnp.where(lax.broadcasted_iota(jnp.int32, (tq, hd), 1) == 0, 1.0, 0.0).astype(BF16)

    def block(j, carry, diagonal):
        off = pl.multiple_of(j * tq, tq)
        out = []
        for g in range(hps):
            m, acc = carry[g]
            cols = slice(g * hd, (g + 1) * hd)
            kt = jnp.concatenate([k_ref[pl.ds(off, tq), cols], ka_ref[pl.ds(off, tq), cols]], axis=1)
            vt = jnp.concatenate([v_ref[pl.ds(off, tq), cols], ones_col], axis=1)
            s = lax.dot_general(qts[g], kt, (((1,), (1,)), ((), ())), preferred_element_type=F32)
            if diagonal:
                r = lax.broadcasted_iota(jnp.int32, (tq, tq), 0)
                c = lax.broadcasted_iota(jnp.int32, (tq, tq), 1)
                s = jnp.where(c <= r, s, NEG_INF)
            m_new = jnp.maximum(m, jnp.max(s, axis=1, keepdims=True))
            alpha = jnp.exp(m - m_new)
            p = jnp.exp(s - m_new)
            acc = acc * alpha + jnp.dot(p.astype(BF16), vt, preferred_element_type=F32)
            out.append((m_new, acc))
        return tuple(out)

    init = tuple((jnp.full((tq, 1), NEG_INF, F32), jnp.zeros((tq, 2 * hd), F32)) for _ in range(hps))
    carry = lax.fori_loop(0, qi, lambda j, c: block(j, c, False), init)
    carry = block(qi, carry, True)
    for g in range(hps):
        acc = carry[g][1]
        o_ref[:, g * hd:(g + 1) * hd] = (acc[:, :hd] / acc[:, hd:hd + 1]).astype(o_ref.dtype)


def _fox_attn_prompt(q, qa, k, ka, v, bsz, seq, nh, hd, tq=512, hps=4):
    qblk = pl.BlockSpec((None, tq, hps * hd), lambda b, h, i: (b, i, h))
    kblk = pl.BlockSpec((None, seq, hps * hd), lambda b, h, i: (b, 0, h))
    return pl.pallas_call(
        functools.partial(_fox_attn_kernel, tq=tq, hd=hd),
        grid=(bsz, nh // hps, seq // tq),
        in_specs=[qblk, qblk, kblk, kblk, kblk],
        out_specs=qblk,
        out_shape=jax.ShapeDtypeStruct((bsz, seq, nh * hd), BF16),
        compiler_params=_cparams(("arbitrary", "arbitrary", "arbitrary")),
        name="fox_attn_prompt",
    )(q, qa, k, ka, v)


def _route(g_logits, e_logits):
    m = g_logits.shape[0]
    gi = lax.broadcasted_iota(jnp.int32, (m, N_GROUPS), 1)
    ei = lax.broadcasted_iota(jnp.int32, (m, N_EXPERTS), 1)
    gmax = jnp.max(g_logits, axis=1, keepdims=True)
    g_idx = jnp.min(jnp.where(g_logits == gmax, gi, N_GROUPS), axis=1, keepdims=True)
    g_gate = 1.0 / jnp.sum(jnp.exp(g_logits - gmax), axis=1, keepdims=True)
    in_group = (ei // EXP_PER_GROUP) == g_idx
    e1 = jnp.where(in_group, e_logits, -jnp.inf)
    top1 = jnp.max(e1, axis=1, keepdims=True)
    i1 = jnp.min(jnp.where(e1 == top1, ei, N_EXPERTS), axis=1, keepdims=True)
    e2 = jnp.where(ei == i1, -jnp.inf, e1)
    top2 = jnp.max(e2, axis=1, keepdims=True)
    i2 = jnp.min(jnp.where(e2 == top2, ei, N_EXPERTS), axis=1, keepdims=True)
    r = jnp.exp(top2 - top1)
    w1 = g_gate / (1.0 + r)
    w2 = g_gate * r / (1.0 + r)
    comb = jnp.where(ei == i1, w1, 0.0) + jnp.where(ei == i2, w2, 0.0)
    two = lax.broadcasted_iota(jnp.int32, (m, 2), 1)
    ids = jnp.where(two == 0, i1, i2)
    wts = jnp.where(two == 0, w1, w2)
    return comb, ids, wts


def _router_kernel(x_ref, g_ref, sh_ref, sc_ref, wg_ref, bg_ref, we_ref, be_ref, h_ref, comb_ref, ids_ref, wts_ref):
    y = _rmsnorm(x_ref[...], g_ref[...])
    h = y * (1.0 + sc_ref[...]) + sh_ref[...]
    h_ref[...] = h.astype(h_ref.dtype)
    g_logits = _dot_lp(h, wg_ref[...]) + bg_ref[...]
    e_logits = _dot_lp(h, we_ref[...]) + be_ref[...]
    comb, ids, wts = _route(g_logits, e_logits)
    comb_ref[...] = comb
    ids_ref[...] = ids
    wts_ref[...] = wts


def _router(x, g, sh_spec, sc_spec, mod_arrays, w_grp, b_grp, w_exp, b_exp, tm, h_dtype):
    t, d = x.shape
    full = lambda shape: pl.BlockSpec(shape, lambda i: (0,) * len(shape))
    row = lambda n: pl.BlockSpec((tm, n), lambda i: (i, 0))
    return pl.pallas_call(
        _router_kernel,
        grid=(t // tm,),
        in_specs=[row(d), full((1, d)), sh_spec, sc_spec,
                  full((d, N_GROUPS)), full((1, N_GROUPS)), full((d, N_EXPERTS)), full((1, N_EXPERTS))],
        out_specs=(row(d), row(N_EXPERTS), row(2), row(2)),
        out_shape=(jax.ShapeDtypeStruct((t, d), h_dtype), jax.ShapeDtypeStruct((t, N_EXPERTS), F32),
                   jax.ShapeDtypeStruct((t, 2), jnp.int32), jax.ShapeDtypeStruct((t, 2), F32)),
        compiler_params=_cparams(("arbitrary",)),
        name="moe_router",
    )(x, g.reshape(1, d), *mod_arrays, w_grp, b_grp.reshape(1, N_GROUPS), w_exp, b_exp.reshape(1, N_EXPERTS))


MOE_TB = 512
MOE_CH = BF16_ROWS
MOE_CPT = 40
MOE_NCH = (2 * MOE_TB + N_EXPERTS * (MOE_CH - 1)) // MOE_CH + 1
MOE_NR = MOE_NCH * MOE_CH
MOE_ROUTE_ROWS = 24


def _route_sort_kernel(x_ref, g_ref, sh_ref, sc_ref, wt_ref, bt_ref, srt_ref, pos_ref, nch_ref, yz_ref):
    yz_ref[...] = jnp.zeros(yz_ref.shape, yz_ref.dtype)
    tb, d = x_ref.shape
    nr = srt_ref.shape[0]
    h = _rmsnorm(x_ref[...], g_ref[...]) * (1.0 + sc_ref[...]) + sh_ref[...]
    hb = h.astype(BF16)
    lg = lax.dot_general(wt_ref[...].astype(BF16), hb, (((1,), (1,)), ((), ())), preferred_element_type=F32) + bt_ref[...]
    grow = lax.broadcasted_iota(jnp.int32, (8, tb), 0)
    gl = jnp.where(grow < N_GROUPS, lg[0:8], -jnp.inf)
    el = lg[8:8 + N_EXPERTS]
    erow = lax.broadcasted_iota(jnp.int32, (N_EXPERTS, tb), 0)
    gmax = jnp.max(gl, axis=0, keepdims=True)
    g_idx = jnp.min(jnp.where(gl == gmax, grow, 8), axis=0, keepdims=True)
    g_gate = 1.0 / jnp.sum(jnp.exp(gl - gmax), axis=0, keepdims=True)
    e1 = jnp.where(erow // EXP_PER_GROUP == g_idx, el, -jnp.inf)
    top1 = jnp.max(e1, axis=0, keepdims=True)
    i1 = jnp.min(jnp.where(e1 == top1, erow, N_EXPERTS), axis=0, keepdims=True)
    e2 = jnp.where(erow == i1, -jnp.inf, e1)
    top2 = jnp.max(e2, axis=0, keepdims=True)
    i2 = jnp.min(jnp.where(e2 == top2, erow, N_EXPERTS), axis=0, keepdims=True)
    r = jnp.exp(top2 - top1)
    w1 = g_gate / (1.0 + r)
    w2 = g_gate * r / (1.0 + r)

    oh = jnp.where(erow == i1, 1.0, jnp.where(erow == i2, 1.0, 0.0))
    nch = (jnp.sum(oh, axis=1, keepdims=True).astype(jnp.int32) + (MOE_CH - 1)) // MOE_CH
    lr = lax.broadcasted_iota(jnp.int32, (N_EXPERTS, N_EXPERTS), 0)
    lc = lax.broadcasted_iota(jnp.int32, (N_EXPERTS, N_EXPERTS), 1)
    before_e = jnp.where(lc < lr, 1.0, 0.0).astype(BF16)
    nch_b = jnp.broadcast_to(nch.astype(F32), (N_EXPERTS, LANES)).astype(BF16)
    off_rows = jnp.dot(before_e, nch_b, preferred_element_type=F32)[:, 0:1] * MOE_CH
    tr = lax.broadcasted_iota(jnp.int32, (tb, tb), 0)
    tc = lax.broadcasted_iota(jnp.int32, (tb, tb), 1)
    before_t = jnp.where(tr < tc, 1.0, 0.0).astype(BF16)
    rank = jnp.dot(oh.astype(BF16), before_t, preferred_element_type=F32)
    row_of = off_rows + rank
    p1 = jnp.sum(jnp.where(erow == i1, row_of, 0.0), axis=0, keepdims=True).astype(jnp.int32)
    p2 = jnp.sum(jnp.where(erow == i2, row_of, 0.0), axis=0, keepdims=True).astype(jnp.int32)

    rowi = lax.broadcasted_iota(jnp.int32, (nr, tb), 0)
    hit1 = rowi == p1
    hit2 = rowi == p2
    pm = jnp.where(hit1, 1.0, jnp.where(hit2, 1.0, 0.0)).astype(BF16)
    cw = 512
    for c0 in range(0, d, cw):
        srt_ref[:, c0:c0 + cw] = jnp.dot(pm, hb[:, c0:c0 + cw], preferred_element_type=F32).astype(BF16)
    wcol = jnp.sum(jnp.where(hit1, w1, jnp.where(hit2, w2, 0.0)), axis=1, keepdims=True)
    rnd = lambda v: v.astype(BF16).astype(F32)
    hi = rnd(wcol)
    mid = rnd(wcol - hi)
    lo = (wcol - hi) - mid
    lane = lax.broadcasted_iota(jnp.int32, (nr, LANES), 1)
    extra = jnp.where(lane == 0, hi, jnp.where(lane == 1, mid, jnp.where(lane == 2, lo, 0.0)))
    srt_ref[:, d:] = extra.astype(BF16)
    prow = lax.broadcasted_iota(jnp.int32, (8, tb), 0)
    pos_ref[...] = jnp.where(prow == 0, p1, jnp.where(prow == 1, p2, 0))
    nch_ref[...] = jnp.broadcast_to(nch, (N_EXPERTS, LANES))


def _route_sort_prompt(x, g, m4, layer, w_grp, b_grp, w_exp, b_exp, seq):
    t, d = x.shape
    nblk = t // MOE_TB
    tpb = seq // MOE_TB
    pad = jnp.zeros((8 - N_GROUPS, d), F32)
    wt = jnp.concatenate([w_grp.T, pad, w_exp.T], axis=0)
    bt = jnp.concatenate([b_grp, jnp.zeros((8 - N_GROUPS,), F32), b_exp]).reshape(MOE_ROUTE_ROWS, 1)
    full = lambda shape: pl.BlockSpec(shape, lambda i: (0,) * len(shape))
    n_steps = _moe_result_shape(nblk, d)[0] // MOE_NR
    assert n_steps * MOE_NR == _moe_result_shape(nblk, d)[0]
    blk = lambda i: jnp.minimum(i, nblk - 1)
    mod = lambda slot: pl.BlockSpec((None, None, 1, d), lambda i: (layer, blk(i) // tpb, 0, slot))
    return pl.pallas_call(
        _route_sort_kernel,
        grid=(n_steps,),
        in_specs=[pl.BlockSpec((MOE_TB, d), lambda i: (blk(i), 0)), full((1, d)), mod(3), mod(4),
                  full((MOE_ROUTE_ROWS, d)), full((MOE_ROUTE_ROWS, 1))],
        out_specs=(pl.BlockSpec((MOE_NR, d + LANES), lambda i: (blk(i), 0)),
                   pl.BlockSpec((None, 8, MOE_TB), lambda i: (blk(i), 0, 0)),
                   pl.BlockSpec((None, N_EXPERTS, LANES), lambda i: (blk(i), 0, 0)),
                   pl.BlockSpec((MOE_NR, d), lambda i: (i, 0))),
        out_shape=(jax.ShapeDtypeStruct((nblk * MOE_NR, d + LANES), BF16),
                   jax.ShapeDtypeStruct((nblk, 8, MOE_TB), jnp.int32),
                   jax.ShapeDtypeStruct((nblk, N_EXPERTS, LANES), jnp.int32),
                   jax.ShapeDtypeStruct(_moe_result_shape(nblk, d), BF16)),
        compiler_params=_cparams(("arbitrary",)),
        name="moe_route_sort",
    )(x, g.reshape(1, d), m4, m4, wt, bt)


def _moe_result_shape(nblk, d):
    return (nblk * MOE_NR + 2 * MOE_CPT * MOE_CH, d)


def _moe_n_tiles(nblk):
    return (nblk * (MOE_NCH - 1)) // MOE_CPT + N_EXPERTS


def _moe_schedule(nch):
    nblk = nch.shape[0]
    n_tiles = _moe_n_tiles(nblk)
    off_c = jnp.cumsum(nch, axis=1) - nch
    n_e = nch.sum(axis=0)
    tiles_e = (n_e + MOE_CPT - 1) // MOE_CPT
    tile_end = jnp.cumsum(tiles_e)
    tile_start = tile_end - tiles_e
    tau = jnp.arange(n_tiles, dtype=jnp.int32)
    e = jnp.minimum((tau[:, None] >= tile_end[None, :]).sum(axis=1), N_EXPERTS - 1).astype(jnp.int32)
    oe = (e[:, None] == jnp.arange(N_EXPERTS, dtype=jnp.int32)[None, :]).astype(jnp.int32)
    pick_e = lambda table: (oe * table[None, :]).sum(axis=1)
    rows_e = lambda table: (oe[:, :, None] * table.T[None, :, :]).sum(axis=1)
    g = (tau - pick_e(tile_start))[:, None] * MOE_CPT + jnp.arange(MOE_CPT, dtype=jnp.int32)[None, :]
    valid = (tau < tile_end[-1])[:, None] & (g < pick_e(n_e)[:, None])
    nch_e = rows_e(nch)
    cum_b = jnp.cumsum(nch_e, axis=1)
    b = jnp.minimum((g[:, :, None] >= cum_b[:, None, :]).sum(axis=-1), nblk - 1)
    ob = (b[:, :, None] == jnp.arange(nblk, dtype=jnp.int32)[None, None, :]).astype(jnp.int32)
    pick_b = lambda table: (ob * table[:, None, :]).sum(axis=-1)
    chunk = b * MOE_NCH + pick_b(rows_e(off_c)) + (g - pick_b(cum_b - nch_e))
    src = jnp.where(valid, chunk, MOE_NCH - 1)
    spare = nblk * MOE_NCH + (tau % 2)[:, None] * MOE_CPT + jnp.arange(MOE_CPT, dtype=jnp.int32)[None, :]
    dst = jnp.where(valid, chunk, spare)
    return (e, src.reshape(-1).astype(jnp.int32), dst.reshape(-1).astype(jnp.int32),
            tile_end[-1:].astype(jnp.int32))


def _moe_expert_kernel(te_ref, src_ref, dst_ref, nv_ref, srt_hbm, w1_ref, w3_ref, w2_ref, yin_hbm, y_hbm,
                       xbuf, ybuf, w1b, w3b, w2b, sem_in, sem_out, *, n_tiles, d):
    del yin_hbm
    t = pl.program_id(0)
    slot = t % 2

    def in_copy(tile, s, c):
        row = pl.multiple_of(src_ref[tile * MOE_CPT + c] * MOE_CH, MOE_CH)
        return pltpu.make_async_copy(srt_hbm.at[pl.ds(row, MOE_CH), :], xbuf.at[s, pl.ds(c * MOE_CH, MOE_CH), :],
                                     sem_in.at[s])

    def out_copy(tile, s, c):
        row = pl.multiple_of(dst_ref[tile * MOE_CPT + c] * MOE_CH, MOE_CH)
        return pltpu.make_async_copy(ybuf.at[s, pl.ds(c * MOE_CH, MOE_CH), :], y_hbm.at[pl.ds(row, MOE_CH), :],
                                     sem_out.at[s])

    @pl.when(t == 0)
    def _():
        for c in range(MOE_CPT):
            in_copy(0, 0, c).start()

    @pl.when(t + 1 < n_tiles)
    def _():
        for c in range(MOE_CPT):
            in_copy(t + 1, 1 - slot, c).start()

    for c in range(MOE_CPT):
        in_copy(t, slot, c).wait()

    @pl.when(t >= 2)
    def _():
        for c in range(MOE_CPT):
            out_copy(t - 2, slot, c).wait()

    @pl.when((t == 0) | (te_ref[t] != te_ref[jnp.maximum(t - 1, 0)]))
    def _():
        w1b[...] = w1_ref[...].astype(BF16)
        w3b[...] = w3_ref[...].astype(BF16)
        w2b[...] = w2_ref[...].astype(BF16)

    @pl.when(t < nv_ref[0])
    def _():
        x = xbuf[slot]
        h = x[:, :d]
        wcol = jnp.sum(x[:, d:].astype(F32), axis=1, keepdims=True)
        a = jnp.dot(h, w1b[...], preferred_element_type=F32)
        b = jnp.dot(h, w3b[...], preferred_element_type=F32)
        hid = (_silu(a) * b * wcol).astype(BF16)
        ybuf[slot] = jnp.dot(hid, w2b[...], preferred_element_type=F32).astype(BF16)

    for c in range(MOE_CPT):
        out_copy(t, slot, c).start()

    @pl.when(t == n_tiles - 1)
    def _():
        for c in range(MOE_CPT):
            out_copy(t, slot, c).wait()
        if n_tiles >= 2:
            for c in range(MOE_CPT):
                out_copy(t - 1, 1 - slot, c).wait()


def _moe_experts(srt, y_zero, te, src, dst, n_used, w1, w3, w2, layer, nblk):
    d = srt.shape[1] - LANES
    f = w1.shape[-1]
    n_tiles = _moe_n_tiles(nblk)
    tile_rows = MOE_CPT * MOE_CH
    y_rows = _moe_result_shape(nblk, d)[0]
    assert y_zero.shape == (y_rows, d)
    wspec = lambda shape: pl.BlockSpec((None, None) + shape, lambda t, te, src, dst, nv: (layer, te[t], 0, 0))
    return pl.pallas_call(
        functools.partial(_moe_expert_kernel, n_tiles=n_tiles, d=d),
        grid_spec=pltpu.PrefetchScalarGridSpec(
            num_scalar_prefetch=4,
            grid=(n_tiles,),
            in_specs=[pl.BlockSpec(memory_space=pl.ANY), wspec((d, f)), wspec((d, f)), wspec((f, d)),
                      pl.BlockSpec(memory_space=pl.ANY)],
            out_specs=pl.BlockSpec(memory_space=pl.ANY),
            scratch_shapes=[pltpu.VMEM((2, tile_rows, d + LANES), BF16), pltpu.VMEM((2, tile_rows, d), BF16),
                            pltpu.VMEM((d, f), BF16), pltpu.VMEM((d, f), BF16), pltpu.VMEM((f, d), BF16),
                            pltpu.SemaphoreType.DMA((2,)), pltpu.SemaphoreType.DMA((2,))]),
        out_shape=jax.ShapeDtypeStruct((y_rows, d), BF16),
        input_output_aliases={8: 0},
        compiler_params=_cparams(("arbitrary",)),
        name="moe_experts",
    )(te, src, dst, n_used, srt, w1, w3, w2, y_zero)


def _unsort_kernel(*refs, last):
    if last:
        y_ref, pos_ref, x_ref, gate_ref, g_ref, out_ref, xn_ref = refs
    else:
        y_ref, pos_ref, x_ref, gate_ref, g_ref, sh_ref, sc_ref, xn_ref, h_ref = refs
    nr = y_ref.shape[0]
    tb, d = x_ref.shape
    rowi = lax.broadcasted_iota(jnp.int32, (nr, tb), 0)
    pm = jnp.where(rowi == pos_ref[0:1, :], 1.0, jnp.where(rowi == pos_ref[1:2, :], 1.0, 0.0)).astype(BF16)
    cw = 512
    for c0 in range(0, d, cw):
        y = lax.dot_general(pm, y_ref[:, c0:c0 + cw], (((0,), (0,)), ((), ())), preferred_element_type=F32)
        xn_ref[:, c0:c0 + cw] = x_ref[:, c0:c0 + cw] + gate_ref[:, c0:c0 + cw] * y
    normed = _rmsnorm(xn_ref[...], g_ref[...])
    if last:
        out_ref[...] = normed
    else:
        h_ref[...] = (normed * (1.0 + sc_ref[...]) + sh_ref[...]).astype(h_ref.dtype)


def _moe_unsort(y_srt, pos, x, m4, layer, seq, g_next, last):
    t, d = x.shape
    tpb = seq // MOE_TB
    rows = pl.BlockSpec((MOE_TB, d), lambda i: (i, 0))
    ins = [y_srt, pos, x, m4, g_next.reshape(1, d)]
    in_specs = [pl.BlockSpec((MOE_NR, d), lambda i: (i, 0)), pl.BlockSpec((None, 8, MOE_TB), lambda i: (i, 0, 0)),
                rows, _mod_spec(layer, 5, tpb, d), pl.BlockSpec((1, d), lambda i: (0, 0))]
    if last:
        out_shape, out_specs = jax.ShapeDtypeStruct((t, d), F32), rows
        scratch = [pltpu.VMEM((MOE_TB, d), F32)]
    else:
        ins += [m4, m4]
        in_specs += [_mod_spec(layer + 1, 0, tpb, d), _mod_spec(layer + 1, 1, tpb, d)]
        out_shape = (jax.ShapeDtypeStruct((t, d), F32), jax.ShapeDtypeStruct((t, d), BF16))
        out_specs, scratch = (rows, rows), []
    return pl.pallas_call(
        functools.partial(_unsort_kernel, last=last),
        grid=(t // MOE_TB,),
        in_specs=in_specs, out_specs=out_specs, out_shape=out_shape, scratch_shapes=scratch,
        compiler_params=_cparams(("arbitrary",)),
        name="moe_unsort",
    )(*ins)


def _s_in_kernel(*refs, act, has_bias):
    if has_bias:
        x_ref, g_ref, sh_ref, sc_ref, w_ref, b_ref, o_ref = refs
    else:
        x_ref, g_ref, sh_ref, sc_ref, w_ref, o_ref = refs
    h = _rmsnorm(x_ref[...], g_ref[...]) * (1.0 + sc_ref[...]) + sh_ref[...]
    y = _dot_lp(h, w_ref[...])
    if has_bias:
        y = y + b_ref[...]
    if act == "gelu":
        y = _gelu_tanh(y)
    elif act == "logsig":
        y = _log_sigmoid(y)
    o_ref[...] = y


def _s_in(x, g, sh, sc, w, bias, act, tn):
    m, d = x.shape
    n = w.shape[1]
    tn = min(tn, n)
    full = pl.BlockSpec((m, d), lambda j: (0, 0))
    ins = [x, g.reshape(1, d), sh, sc, w]
    in_specs = [full, pl.BlockSpec((1, d), lambda j: (0, 0)), full, full, pl.BlockSpec((d, tn), lambda j: (0, j))]
    if bias is not None:
        ins.append(bias.reshape(1, n))
        in_specs.append(pl.BlockSpec((1, tn), lambda j: (0, j)))
    return pl.pallas_call(
        functools.partial(_s_in_kernel, act=act, has_bias=bias is not None),
        grid=(n // tn,),
        in_specs=in_specs,
        out_specs=pl.BlockSpec((m, tn), lambda j: (0, j)),
        out_shape=jax.ShapeDtypeStruct((m, n), F32),
        compiler_params=_cparams(("arbitrary",)),
        name="s_in_" + act,
    )(*ins)


def _s_out_kernel(*refs, has_bias):
    if has_bias:
        a_ref, w_ref, b_ref, x_ref, gate_ref, o_ref = refs
    else:
        a_ref, w_ref, x_ref, gate_ref, o_ref = refs
    y = _dot_lp(a_ref[...], w_ref[...])
    if has_bias:
        y = y + b_ref[...]
    o_ref[...] = x_ref[...] + gate_ref[...] * y


def _s_out(a, w, bias, x, gate, tn=512):
    m, k = a.shape
    n = w.shape[1]
    col = pl.BlockSpec((m, tn), lambda j: (0, j))
    ins = [a, w]
    in_specs = [pl.BlockSpec((m, k), lambda j: (0, 0)), pl.BlockSpec((k, tn), lambda j: (0, j))]
    if bias is not None:
        ins.append(bias.reshape(1, n))
        in_specs.append(pl.BlockSpec((1, tn), lambda j: (0, j)))
    ins += [x, gate]
    in_specs += [col, col]
    return pl.pallas_call(
        functools.partial(_s_out_kernel, has_bias=bias is not None),
        grid=(n // tn,),
        in_specs=in_specs, out_specs=col,
        out_shape=jax.ShapeDtypeStruct((m, n), F32),
        compiler_params=_cparams(("arbitrary",)),
        name="s_out",
    )(*ins)


def _s_sgu_kernel(z_ref, lng_ref, lnb_ref, w00_ref, b0_ref, gv_ref, o_ref, *, width):
    u = z_ref[:, :width]
    v = z_ref[:, width:]
    mu = jnp.mean(v, axis=-1, keepdims=True)
    vc = v - mu
    vn = vc * lax.rsqrt(jnp.mean(vc * vc, axis=-1, keepdims=True) + LN_EPS) * lng_ref[...] + lnb_ref[...]
    gv_ref[...] = vn
    o_ref[...] = u * (vn * w00_ref[...] + b0_ref[...])


def _s_sgu(z, ln_g, ln_b, ws, bs):
    m, n2 = z.shape
    width = n2 // 2
    gd = width // GM_GROUPS
    w00 = jnp.repeat(ws[:, 0, 0], gd).reshape(1, width)
    b0 = jnp.repeat(bs[:, 0], gd).reshape(1, width)
    vec = pl.BlockSpec((1, width), lambda: (0, 0))
    blk = pl.BlockSpec((m, width), lambda: (0, 0))
    return pl.pallas_call(
        functools.partial(_s_sgu_kernel, width=width),
        in_specs=[pl.BlockSpec((m, n2), lambda: (0, 0)), vec, vec, vec, vec],
        out_specs=(blk, blk),
        out_shape=(jax.ShapeDtypeStruct((m, width), F32), jax.ShapeDtypeStruct((m, width), F32)),
        name="s_sgu",
    )(z, ln_g.reshape(1, width), ln_b.reshape(1, width), w00, b0)


def _s_attn_kernel(pt_ref, q_ref, kn_ref, vn_ref, lfn_ref, cnf_ref, *refs, n_steps, pps, page, nh, scale):
    del pt_ref
    ck_refs, cv_refs, clf_refs = refs[:pps], refs[pps:2 * pps], refs[2 * pps:3 * pps]
    o_ref, m_ref, l_ref, acc_ref, carry_ref = refs[3 * pps:]
    p = pl.program_id(1)
    pflat = page * nh
    flat = pps * pflat
    lanes = m_ref.shape[1]

    @pl.when(p == 0)
    def _():
        m_ref[...] = jnp.full_like(m_ref, NEG_INF)
        l_ref[...] = jnp.zeros_like(l_ref)
        acc_ref[...] = jnp.zeros_like(acc_ref)
        carry_ref[...] = jnp.zeros_like(carry_ref)

    def per_head(x, op):
        y = x[:, :lanes]
        for i in range(1, flat // lanes):
            y = op(y, x[:, i * lanes:(i + 1) * lanes])
        s = nh
        while s < lanes:
            y = op(y, pltpu.roll(y, s, axis=1))
            s *= 2
        return y

    tile = lambda y: jnp.concatenate([y] * (flat // lanes), axis=1)

    def to_col(row):
        r = lax.broadcasted_iota(jnp.int32, (nh, nh), 0)
        c = lax.broadcasted_iota(jnp.int32, (nh, nh), 1)
        return jnp.sum(jnp.where(r == c, jnp.broadcast_to(row, (nh, nh)), 0.0), axis=1, keepdims=True)

    qb = q_ref[...].astype(BF16)
    own_head = (lax.broadcasted_iota(jnp.int32, (nh, pflat), 1) % nh) == lax.broadcasted_iota(jnp.int32, (nh, pflat), 0)

    lf = jnp.concatenate([r[...] for r in clf_refs], axis=1)
    lane_idx = lax.broadcasted_iota(jnp.int32, (1, flat), 1)
    inc = lf
    step = nh
    while step < flat:
        if step % lanes == 0:
            shifted = jnp.concatenate([inc[:, step:], jnp.zeros((1, step), F32)], axis=1)
        else:
            shifted = jnp.where(lane_idx < flat - step, pltpu.roll(inc, flat - step, axis=1), 0.0)
        inc = inc + shifted
        step *= 2
    rest = (inc - lf) + tile(carry_ref[...])
    carry_ref[...] = carry_ref[...] + per_head(lf, jnp.add)

    def own_logits(k_ref):
        sf = lax.dot_general(qb, k_ref[...].astype(BF16), (((1,), (1,)), ((), ())), preferred_element_type=F32)
        return jnp.sum(jnp.where(own_head, sf, 0.0), axis=0, keepdims=True)

    qk = jnp.concatenate([own_logits(r) for r in ck_refs], axis=1)
    s = qk * scale + jnp.concatenate([cnf_ref[...]] * pps, axis=1) + rest
    m_old = m_ref[...]
    m_new = jnp.maximum(m_old, per_head(s, jnp.maximum))
    alpha = jnp.exp(m_old - m_new)
    pr = jnp.exp(s - tile(m_new))
    l_ref[...] = l_ref[...] * alpha + per_head(pr, jnp.add)
    m_ref[...] = m_new
    pv = None
    for j, v_ref in enumerate(cv_refs):
        pr_j = pr[:, j * pflat:(j + 1) * pflat]
        pm = jnp.where(own_head, jnp.broadcast_to(pr_j, (nh, pflat)), 0.0).astype(BF16)
        d = jnp.dot(pm, v_ref[...].astype(BF16), preferred_element_type=F32)
        pv = d if pv is None else pv + d
    acc_ref[...] = acc_ref[...] * to_col(alpha[:, :nh]) + pv

    @pl.when(p == n_steps - 1)
    def _():
        rnd = lambda x: x.astype(BF16).astype(F32)
        c_new = lfn_ref[...]
        sn = lax.dot_general(qb, kn_ref[...].astype(BF16), (((1,), (1,)), ((), ())), preferred_element_type=F32)
        r = lax.broadcasted_iota(jnp.int32, (nh, nh), 0)
        c = lax.broadcasted_iota(jnp.int32, (nh, nh), 1)
        s_n = jnp.sum(jnp.where(r == c, sn, 0.0), axis=0, keepdims=True) * scale + (c_new - c_new)
        m_o = m_ref[:, :nh]
        m_n = jnp.maximum(m_o, s_n)
        alpha_n = jnp.exp(m_o - m_n)
        pr_n = jnp.exp(s_n - m_n)
        l_n = l_ref[:, :nh] * alpha_n + pr_n
        acc = acc_ref[...] * to_col(alpha_n) + to_col(rnd(pr_n)) * rnd(vn_ref[...])
        o_ref[...] = acc / to_col(l_n)


def _s_attn(q, k_new, v_new, lf_new, cache_k, cache_v, cache_logf, page_table, slot):
    bsz, d = q.shape
    n_fox, n_pool, page, nh, hd = cache_k.shape
    n_pages = page_table.shape[1]
    flat = page * nh
    pps = next(c for c in (4, 2, 1) if n_pages % c == 0)
    n_steps = n_pages // pps
    heads = pl.BlockSpec((None, nh, hd), lambda b, p, pt: (b, 0, 0))
    per_seq = lambda n: pl.BlockSpec((None, 1, n), lambda b, p, pt: (b, 0, 0))

    def pg_idx(j):
        return lambda b, p, pt: (slot, pt[b * n_pages + (n_steps - 1 - p) * pps + j], 0, 0)

    cache_k = cache_k.reshape(n_fox, n_pool, flat, hd)
    cache_v = cache_v.reshape(n_fox, n_pool, flat, hd)
    clf_flat = cache_logf.reshape(n_fox, n_pool, 1, flat)
    cn_flat = jnp.tile(lf_new, (1, page)).reshape(bsz, 1, flat)
    kv_specs = [pl.BlockSpec((None, None, flat, hd), pg_idx(j)) for j in range(pps)]
    lf_specs = [pl.BlockSpec((None, None, 1, flat), pg_idx(j)) for j in range(pps)]
    out = pl.pallas_call(
        functools.partial(_s_attn_kernel, n_steps=n_steps, pps=pps, page=page, nh=nh, scale=hd ** -0.5),
        grid_spec=pltpu.PrefetchScalarGridSpec(
            num_scalar_prefetch=1,
            grid=(bsz, n_steps),
            in_specs=[heads, heads, heads, per_seq(nh), per_seq(flat)] + kv_specs + kv_specs + lf_specs,
            out_specs=heads,
            scratch_shapes=[pltpu.VMEM((1, LANES), F32), pltpu.VMEM((1, LANES), F32), pltpu.VMEM((nh, hd), F32),
                            pltpu.VMEM((1, LANES), F32)]),
        out_shape=jax.ShapeDtypeStruct((bsz, nh, hd), F32),
        compiler_params=_cparams(("arbitrary", "arbitrary")),
        name="s_attn",
    )(page_table.reshape(-1), q.reshape(bsz, nh, hd), k_new.reshape(bsz, nh, hd), v_new.reshape(bsz, nh, hd),
      lf_new.reshape(bsz, 1, nh), cn_flat, *([cache_k] * pps), *([cache_v] * pps), *([clf_flat] * pps))
    return out.reshape(bsz, d)


def _s_moe_kernel(eid_ref, pair_ref, h_ref, wts_ref, w1_ref, w3_ref, w2_ref, x_ref, gate_ref, o_ref, acc_ref, *, n_pairs):
    del eid_ref
    p = pl.program_id(0)
    pair = pair_ref[p]

    @pl.when(p == 0)
    def _():
        acc_ref[...] = jnp.zeros_like(acc_ref)

    h = h_ref[...]
    m = h.shape[0]
    a = _dot_lp(h, w1_ref[...])
    b = _dot_lp(h, w3_ref[...])
    wts = wts_ref[...]
    wcol = jnp.where(pair % 2 == 0, wts[:, 0:1], wts[:, 1:2])
    wcol = jnp.where(lax.broadcasted_iota(jnp.int32, (m, 1), 0) == pair // 2, wcol, 0.0)
    acc_ref[...] += _dot_lp(_silu(a) * b * wcol, w2_ref[...])

    @pl.when(p == n_pairs - 1)
    def _():
        o_ref[...] = x_ref[...] + gate_ref[...] * acc_ref[...]


def _s_moe(h, ids, wts, w1, w3, w2, x, gate, layer):
    m, d = h.shape
    f = w1.shape[-1]
    n_pairs = 2 * m
    eid = ids.reshape(-1)
    idx = jnp.arange(n_pairs, dtype=jnp.int32)
    rank = ((eid[None, :] < eid[:, None]) | ((eid[None, :] == eid[:, None]) & (idx[None, :] < idx[:, None]))).sum(axis=1)
    hit = rank[None, :] == idx[:, None]
    order = (hit * idx[None, :]).sum(axis=1).astype(jnp.int32)
    eid_sorted = (hit * eid[None, :]).sum(axis=1).astype(jnp.int32)
    full = lambda shape: pl.BlockSpec(shape, lambda p, eid, pair: (0,) * len(shape))
    wspec = lambda shape: pl.BlockSpec((None, None) + shape, lambda p, eid, pair: (layer, eid[p], 0, 0))
    return pl.pallas_call(
        functools.partial(_s_moe_kernel, n_pairs=n_pairs),
        grid_spec=pltpu.PrefetchScalarGridSpec(
            num_scalar_prefetch=2,
            grid=(n_pairs,),
            in_specs=[full((m, d)), full((m, 2)), wspec((d, f)), wspec((d, f)), wspec((f, d)),
                      full((m, d)), full((m, d))],
            out_specs=full((m, d)),
            scratch_shapes=[pltpu.VMEM((m, d), F32)]),
        out_shape=jax.ShapeDtypeStruct((m, d), F32),
        compiler_params=_cparams(("arbitrary",)),
        name="s_moe",
    )(eid_sorted, order, h, wts, w1, w3, w2, x, gate)


def kernel(x_prompt, x_sample, cache_k, cache_v, cache_logf, page_table, c_prompt, c_sample,
           w_ada, b_ada, norm1_g, norm2_g, final_g,
           gm_w_in, gm_b_in, gm_ln_g, gm_ln_b, gm_ws, gm_bs, gm_w_out, gm_b_out,
           fox_w_qkv, fox_w_f, fox_b_f, fox_w_o,
           moe_w_grp, moe_b_grp, moe_w_exp, moe_b_exp, moe_w1, moe_w3, moe_w2):
    bsz, seq, d = x_prompt.shape
    dbsz = x_sample.shape[0]
    depth = w_ada.shape[0]
    nh, hd = cache_k.shape[3], cache_k.shape[4]
    t = bsz * seq
    assert bsz + dbsz <= ADA_ROWS and x_sample.shape[1] == 1

    c16 = jnp.concatenate([c_prompt, c_sample, jnp.zeros((ADA_ROWS - bsz - dbsz, d), F32)], axis=0)
    m_all = _ada_all(c16, w_ada, b_ada)
    m4 = m_all.reshape(depth, ADA_ROWS, 1, 6 * d)

    xp = x_prompt.reshape(t, d)
    xs = x_sample.reshape(dbsz, d)
    k_p, v_p, lf_p, k_s, v_s, lf_s, gv_s = [], [], [], [], [], [], []
    hp = _norm_mod_prompt(xp, norm1_g[0], m4, 0, 0, seq)
    y_prompt = None

    for i in range(depth):
        slot = i // 2
        ms = m_all[i, bsz:bsz + dbsz].reshape(dbsz, 6, d)
        if i % 2 == 0:
            z = _mm_prompt("gelu", hp, gm_w_in[slot], bias=gm_b_in[slot])
            gated = _sgu_prompt(z, gm_ln_g[slot], gm_ln_b[slot], gm_ws[slot], gm_bs[slot])
            xp = _mm_prompt("resid", gated, gm_w_out[slot], bias=gm_b_out[slot], x=xp, m4=m4, layer=i, gate_slot=2,
                            seq=seq)
            zs = _s_in(xs, norm1_g[i], ms[:, 0], ms[:, 1], gm_w_in[slot], gm_b_in[slot], "gelu", 512)
            gv, gs = _s_sgu(zs, gm_ln_g[slot], gm_ln_b[slot], gm_ws[slot], gm_bs[slot])
            gv_s.append(gv)
            xs = _s_out(gs, gm_w_out[slot], gm_b_out[slot], xs, ms[:, 2])
        else:
            wqkv = fox_w_qkv[slot]
            q = _mm_prompt("proj_bf16", hp, wqkv, col0=0, n=d, out_scale=hd ** -0.5)
            kf, kb = _mm_prompt("proj_both", hp, wqkv, col0=d, n=d)
            vf, vb = _mm_prompt("proj_both", hp, wqkv, col0=2 * d, n=d)
            lf, qa, ka = _logf_prompt(hp, fox_w_f[slot], fox_b_f[slot], bsz, seq, hd)
            b3 = lambda a: a.reshape(bsz, seq, d)
            o = _fox_attn_prompt(b3(q), b3(qa), b3(kb), b3(ka), b3(vb), bsz, seq, nh, hd)
            xp = _mm_prompt("resid", o.reshape(t, d), fox_w_o[slot], bias=jnp.zeros((d,), F32), x=xp, m4=m4, layer=i,
                            gate_slot=2, seq=seq)
            k_p.append(kf.reshape(bsz, seq, nh, hd))
            v_p.append(vf.reshape(bsz, seq, nh, hd))
            lf_p.append(lf.reshape(bsz, seq, nh))
            qkv = _s_in(xs, norm1_g[i], ms[:, 0], ms[:, 1], wqkv, None, "none", 512)
            lfn = _s_in(xs, norm1_g[i], ms[:, 0], ms[:, 1], fox_w_f[slot], fox_b_f[slot], "logsig", 512)
            qs, kn, vn = qkv[:, :d], qkv[:, d:2 * d], qkv[:, 2 * d:]
            o_s = _s_attn(qs, kn, vn, lfn, cache_k, cache_v, cache_logf, page_table, slot)
            xs = _s_out(o_s, fox_w_o[slot], None, xs, ms[:, 2])
            k_s.append(kn.reshape(dbsz, 1, nh, hd))
            v_s.append(vn.reshape(dbsz, 1, nh, hd))
            lf_s.append(lfn.reshape(dbsz, 1, nh))

        srt, pos, nch, y_zero = _route_sort_prompt(xp, norm2_g[i], m4, i, moe_w_grp[i], moe_b_grp[i], moe_w_exp[i],
                                                   moe_b_exp[i], seq)
        te, src, dst, n_used = _moe_schedule(nch[:, :, 0])
        y_srt = _moe_experts(srt, y_zero, te, src, dst, n_used, moe_w1, moe_w3, moe_w2, i, t // MOE_TB)
        if i + 1 < depth:
            xp, hp = _moe_unsort(y_srt, pos, xp, m4, i, seq, norm1_g[i + 1], last=False)
        else:
            y_prompt = _moe_unsort(y_srt, pos, xp, m4, i, seq, final_g, last=True).reshape(bsz, seq, d)
        full8 = pl.BlockSpec((dbsz, d), lambda j: (0, 0))
        h2s, _, ids, wts = _router(xs, norm2_g[i], full8, full8, (ms[:, 3], ms[:, 4]), moe_w_grp[i], moe_b_grp[i],
                                   moe_w_exp[i], moe_b_exp[i], dbsz, F32)
        xs = _s_moe(h2s, ids, wts, moe_w1, moe_w3, moe_w2, xs, ms[:, 5], i)

    y_sample = _final_norm(xs, final_g, dbsz).reshape(dbsz, 1, d)
    return (y_prompt, y_sample, jnp.stack(k_p), jnp.stack(v_p), jnp.stack(lf_p),
            jnp.stack(k_s), jnp.stack(v_s), jnp.stack(lf_s), jnp.stack(gv_s).reshape(len(gv_s), dbsz, 1, d))
```

```python
import functools
import math

import jax
import jax.numpy as jnp
from jax import lax
from jax.experimental import pallas as pl
from jax.experimental.pallas import tpu as pltpu

F32 = jnp.float32
BF16 = jnp.bfloat16

RMS_EPS = 1e-6
LN_EPS = 1e-5
NEG_INF = -1e30
CHUNK = 128
GM_GROUPS = 16
N_GROUPS = 4
EXP_PER_GROUP = 4
N_EXPERTS = N_GROUPS * EXP_PER_GROUP
LANES = 128
BF16_ROWS = 16
ADA_ROWS = 16

V7X_VMEM_LIMIT = 56 * 1024 * 1024


def _cparams(sem, vmem=V7X_VMEM_LIMIT):
    return pltpu.CompilerParams(dimension_semantics=sem, vmem_limit_bytes=vmem)


def _silu(x):
    return x / (1.0 + jnp.exp(-x))


def _gelu_tanh(x):
    return 0.5 * x * (1.0 + jnp.tanh(math.sqrt(2.0 / math.pi) * (x + 0.044715 * (x * x * x))))


def _log_sigmoid(x):
    y = -x
    return -(jnp.maximum(y, 0.0) + jnp.log1p(jnp.exp(-jnp.abs(y))))


def _rmsnorm(x, g):
    return x * lax.rsqrt(jnp.mean(x * x, axis=-1, keepdims=True) + RMS_EPS) * g


def _stack_split(a, terms):
    parts = []
    r = a
    for _ in range(terms):
        p = r.astype(BF16).astype(F32)
        parts.append(p)
        r = r - p
    return jnp.concatenate(parts, axis=0).astype(BF16)


def _fold_rows(r, m, terms):
    out = r[:m]
    for t in range(1, terms):
        out = out + r[t * m:(t + 1) * m]
    return out


def _dot_lp(a, b):
    m = a.shape[0]
    pad = (-m) % BF16_ROWS
    if pad:
        a = jnp.concatenate([a, jnp.zeros((pad, a.shape[1]), F32)], axis=0)
    r = jnp.dot(a.astype(BF16), b.astype(BF16), preferred_element_type=F32)
    return r[:m] if pad else r


def _dot_split_exact_rhs(a, b_bf16, terms=2):
    m = a.shape[0]
    mp = m + (-m) % BF16_ROWS
    if mp != m:
        a = jnp.concatenate([a, jnp.zeros((mp - m, a.shape[1]), F32)], axis=0)
    r = jnp.dot(_stack_split(a, terms), b_bf16, preferred_element_type=F32)
    return _fold_rows(r, mp, terms)[:m]


def _dot_exact_lhs_split(a_bf16, b, terms=3):
    out = None
    r = b
    for _ in range(terms):
        p = r.astype(BF16)
        d = jnp.dot(a_bf16, p, preferred_element_type=F32)
        out = d if out is None else out + d
        r = r - p.astype(F32)
    return out


def _ada_kernel(c_ref, w_ref, b_ref, o_ref):
    o_ref[...] = _dot_lp(_silu(c_ref[...]), w_ref[...]) + b_ref[...]


def _ada_all(c16, w_ada, b_ada, tn=1024):
    depth, d, n6 = w_ada.shape
    return pl.pallas_call(
        _ada_kernel,
        grid=(depth, n6 // tn),
        in_specs=[pl.BlockSpec((ADA_ROWS, d), lambda l, j: (0, 0)),
                  pl.BlockSpec((None, d, tn), lambda l, j: (l, 0, j)),
                  pl.BlockSpec((None, 1, tn), lambda l, j: (l, 0, j))],
        out_specs=pl.BlockSpec((None, ADA_ROWS, tn), lambda l, j: (l, 0, j)),
        out_shape=jax.ShapeDtypeStruct((depth, ADA_ROWS, n6), F32),
        compiler_params=_cparams(("arbitrary", "arbitrary")),
        name="ada_all",
    )(c16, w_ada, b_ada.reshape(depth, 1, n6))


def _norm_mod_kernel(x_ref, g_ref, sh_ref, sc_ref, o_ref):
    y = _rmsnorm(x_ref[...], g_ref[...])
    o_ref[...] = (y * (1.0 + sc_ref[...]) + sh_ref[...]).astype(o_ref.dtype)


def _norm_kernel(x_ref, g_ref, o_ref):
    o_ref[...] = _rmsnorm(x_ref[...], g_ref[...]).astype(o_ref.dtype)


def _mod_spec(layer, slot, tiles_per_batch, d):
    return pl.BlockSpec((None, None, 1, d), lambda i, *_: (layer, i // tiles_per_batch, 0, slot))


def _norm_mod_prompt(x, g, m4, layer, slot_shift, seq, tm=512):
    t, d = x.shape
    tpb = seq // tm
    return pl.pallas_call(
        _norm_mod_kernel,
        grid=(t // tm,),
        in_specs=[pl.BlockSpec((tm, d), lambda i: (i, 0)),
                  pl.BlockSpec((1, d), lambda i: (0, 0)),
                  _mod_spec(layer, slot_shift, tpb, d),
                  _mod_spec(layer, slot_shift + 1, tpb, d)],
        out_specs=pl.BlockSpec((tm, d), lambda i: (i, 0)),
        out_shape=jax.ShapeDtypeStruct((t, d), BF16),
        compiler_params=_cparams(("arbitrary",)),
        name="norm_mod_prompt",
    )(x, g.reshape(1, d), m4, m4)


def _final_norm(x, g, tm):
    t, d = x.shape
    return pl.pallas_call(
        _norm_kernel,
        grid=(t // tm,),
        in_specs=[pl.BlockSpec((tm, d), lambda i: (i, 0)), pl.BlockSpec((1, d), lambda i: (0, 0))],
        out_specs=pl.BlockSpec((tm, d), lambda i: (i, 0)),
        out_shape=jax.ShapeDtypeStruct((t, d), F32),
        compiler_params=_cparams(("arbitrary",)),
        name="final_norm",
    )(x, g.reshape(1, d))


def _mm_kernel(*refs, kind, out_scale):
    if kind == "gelu":
        a_ref, w_ref, b_ref, z_ref, wb_ref = refs
    elif kind == "resid":
        a_ref, w_ref, b_ref, x_ref, gate_ref, xo_ref, wb_ref = refs
    elif kind == "proj_bf16":
        a_ref, w_ref, ob_ref, wb_ref = refs
    else:
        a_ref, w_ref, of_ref, ob_ref, wb_ref = refs

    @pl.when(pl.program_id(1) == 0)
    def _():
        wb_ref[...] = w_ref[...].astype(BF16)

    acc = jnp.dot(a_ref[...], wb_ref[...], preferred_element_type=F32)
    if kind == "gelu":
        z_ref[...] = _gelu_tanh(acc + b_ref[...]).astype(z_ref.dtype)
    elif kind == "resid":
        xo_ref[...] = x_ref[...] + gate_ref[...] * (acc + b_ref[...])
    elif kind == "proj_bf16":
        ob_ref[...] = (acc * out_scale).astype(BF16)
    else:
        of_ref[...] = acc
        ob_ref[...] = acc.astype(BF16)


def _mm_prompt(kind, a, w, *, col0=0, n=None, bias=None, x=None, m4=None, layer=None, gate_slot=None, seq=None,
               out_scale=1.0,
               tm=1024, tn=1024):
    t, k = a.shape
    n = w.shape[1] if n is None else n
    jb = col0 // tn
    a_spec = pl.BlockSpec((tm, k), lambda j, i: (i, 0))
    w_spec = pl.BlockSpec((k, tn), lambda j, i: (0, j + jb))
    o_spec = pl.BlockSpec((tm, tn), lambda j, i: (i, j))
    ins, in_specs = [a, w], [a_spec, w_spec]
    if kind in ("gelu", "resid"):
        ins.append(bias.reshape(1, n))
        in_specs.append(pl.BlockSpec((1, tn), lambda j, i: (0, j)))
    if kind == "resid":
        tpb = seq // tm
        ins += [x, m4]
        in_specs += [o_spec, pl.BlockSpec((None, None, 1, tn),
                                          lambda j, i: (layer, i // tpb, 0, gate_slot * (n // tn) + j))]
    if kind == "gelu":
        out_shape, out_specs = jax.ShapeDtypeStruct((t, n), BF16), o_spec
    elif kind == "resid":
        out_shape, out_specs = jax.ShapeDtypeStruct((t, n), F32), o_spec
    elif kind == "proj_bf16":
        out_shape, out_specs = jax.ShapeDtypeStruct((t, n), BF16), o_spec
    else:
        out_shape = (jax.ShapeDtypeStruct((t, n), F32), jax.ShapeDtypeStruct((t, n), BF16))
        out_specs = (o_spec, o_spec)
    return pl.pallas_call(
        functools.partial(_mm_kernel, kind=kind, out_scale=out_scale),
        grid=(n // tn, t // tm),
        in_specs=in_specs, out_specs=out_specs, out_shape=out_shape,
        scratch_shapes=[pltpu.VMEM((k, tn), BF16)],
        compiler_params=_cparams(("arbitrary", "arbitrary")),
        name="mm_" + kind,
    )(*ins)


def _sgu_kernel(z_ref, lng_ref, lnb_ref, ws_ref, bst_ref, o_ref, wt_ref, *, width):
    @pl.when(pl.program_id(0) == 0)
    def _():
        r = lax.broadcasted_iota(jnp.int32, (CHUNK, CHUNK), 0)
        c = lax.broadcasted_iota(jnp.int32, (CHUNK, CHUNK), 1)
        for g in range(GM_GROUPS):
            wt_ref[g] = jnp.where(c <= r, ws_ref[g], 0.0).astype(BF16)

    v = z_ref[:, width:].astype(F32)
    mu = jnp.mean(v, axis=-1, keepdims=True)
    vc = v - mu
    vn = vc * lax.rsqrt(jnp.mean(vc * vc, axis=-1, keepdims=True) + LN_EPS) * lng_ref[...] + lnb_ref[...]
    vb = vn.astype(BF16)
    gd = width // GM_GROUPS
    for g in range(GM_GROUPS):
        lo, hi = g * gd, (g + 1) * gd
        s = jnp.dot(wt_ref[g], vb[:, lo:hi], preferred_element_type=F32) + bst_ref[:, g:g + 1]
        o_ref[:, lo:hi] = (z_ref[:, lo:hi].astype(F32) * s).astype(BF16)


def _sgu_prompt(z, ln_g, ln_b, ws, bs):
    t, n2 = z.shape
    width = n2 // 2
    return pl.pallas_call(
        functools.partial(_sgu_kernel, width=width),
        grid=(t // CHUNK,),
        in_specs=[pl.BlockSpec((CHUNK, n2), lambda i: (i, 0)),
                  pl.BlockSpec((1, width), lambda i: (0, 0)),
                  pl.BlockSpec((1, width), lambda i: (0, 0)),
                  pl.BlockSpec((GM_GROUPS, CHUNK, CHUNK), lambda i: (0, 0, 0)),
                  pl.BlockSpec((CHUNK, GM_GROUPS), lambda i: (0, 0))],
        out_specs=pl.BlockSpec((CHUNK, width), lambda i: (i, 0)),
        out_shape=jax.ShapeDtypeStruct((t, width), BF16),
        scratch_shapes=[pltpu.VMEM((GM_GROUPS, CHUNK, CHUNK), BF16)],
        compiler_params=_cparams(("arbitrary",)),
        name="sgu_prompt",
    )(z, ln_g.reshape(1, width), ln_b.reshape(1, width), ws, bs.T)


def _logf_kernel(h_ref, wf_ref, bf_ref, lf_ref, qa_ref, ka_ref, carry_ref, *, tm, hd):
    @pl.when(pl.program_id(1) == 0)
    def _():
        carry_ref[...] = jnp.zeros_like(carry_ref)

    logit = jnp.dot(h_ref[...], wf_ref[...].astype(BF16), preferred_element_type=F32) + bf_ref[...]
    lf = _log_sigmoid(logit)
    lf_ref[...] = lf
    r = lax.broadcasted_iota(jnp.int32, (tm, tm), 0)
    c = lax.broadcasted_iota(jnp.int32, (tm, tm), 1)
    tri = jnp.where(c <= r, 1.0, 0.0).astype(BF16)
    cum = _dot_exact_lhs_split(tri, lf, 3) + carry_ref[...]
    carry_ref[...] = cum[tm - 1:tm, :]
    lane = lax.broadcasted_iota(jnp.int32, (tm, hd), 1)
    rnd = lambda x: x.astype(BF16).astype(F32)
    for h in range(lf.shape[1]):
        c_h = cum[:, h:h + 1]
        hi = rnd(c_h)
        mid = rnd(c_h - hi)
        lo = (c_h - hi) - mid
        qa = jnp.where(lane == 0, hi, jnp.where(lane == 1, mid, jnp.where(lane == 2, lo, jnp.where(lane < 6, 1.0, 0.0))))
        ka = jnp.where(lane < 3, 1.0, jnp.where(lane == 3, -hi, jnp.where(lane == 4, -mid, jnp.where(lane == 5, -lo, 0.0))))
        qa_ref[:, h * hd:(h + 1) * hd] = qa.astype(BF16)
        ka_ref[:, h * hd:(h + 1) * hd] = ka.astype(BF16)


def _logf_prompt(h, w_f, b_f, bsz, seq, hd, tm=512):
    t, d = h.shape
    nh = w_f.shape[1]
    tpb = seq // tm
    row = lambda n: pl.BlockSpec((tm, n), lambda b, i: (b * tpb + i, 0))
    return pl.pallas_call(
        functools.partial(_logf_kernel, tm=tm, hd=hd),
        grid=(bsz, tpb),
        in_specs=[row(d), pl.BlockSpec((d, nh), lambda b, i: (0, 0)), pl.BlockSpec((1, nh), lambda b, i: (0, 0))],
        out_specs=(row(nh), row(nh * hd), row(nh * hd)),
        out_shape=(jax.ShapeDtypeStruct((t, nh), F32), jax.ShapeDtypeStruct((t, nh * hd), BF16),
                   jax.ShapeDtypeStruct((t, nh * hd), BF16)),
        scratch_shapes=[pltpu.VMEM((1, nh), F32)],
        compiler_params=_cparams(("arbitrary", "arbitrary")),
        name="logf_prompt",
    )(h, w_f, b_f.reshape(1, nh))


def _fox_attn_kernel(q_ref, qa_ref, k_ref, ka_ref, v_ref, o_ref, *, tq, hd):
    qi = pl.program_id(2)
    hps = q_ref.shape[1] // hd
    qts = [jnp.concatenate([q_ref[:, g * hd:(g + 1) * hd], qa_ref[:, g * hd:(g + 1) * hd]], axis=1) for g in range(hps)]
    ones_col = jnp.where(lax.broadcasted_iota(jnp.int32, (tq, hd), 1) == 0, 1.0, 0.0).astype(BF16)

    def block(j, carry, diagonal):
        off = pl.multiple_of(j * tq, tq)
        out = []
        for g in range(hps):
            m, acc = carry[g]
            cols = slice(g * hd, (g + 1) * hd)
            kt = jnp.concatenate([k_ref[pl.ds(off, tq), cols], ka_ref[pl.ds(off, tq), cols]], axis=1)
            vt = jnp.concatenate([v_ref[pl.ds(off, tq), cols], ones_col], axis=1)
            s = lax.dot_general(qts[g], kt, (((1,), (1,)), ((), ())), preferred_element_type=F32)
            if diagonal:
                r = lax.broadcasted_iota(jnp.int32, (tq, tq), 0)
                c = lax.broadcasted_iota(jnp.int32, (tq, tq), 1)
                s = jnp.where(c <= r, s, NEG_INF)
            m_new = jnp.maximum(m, jnp.max(s, axis=1, keepdims=True))
            alpha = jnp.exp(m - m_new)
            p = jnp.exp(s - m_new)
            acc = acc * alpha + jnp.dot(p.astype(BF16), vt, preferred_element_type=F32)
            out.append((m_new, acc))
        return tuple(out)

    init = tuple((jnp.full((tq, 1), NEG_INF, F32), jnp.zeros((tq, 2 * hd), F32)) for _ in range(hps))
    carry = lax.fori_loop(0, qi, lambda j, c: block(j, c, False), init)
    carry = block(qi, carry, True)
    for g in range(hps):
        acc = carry[g][1]
        o_ref[:, g * hd:(g + 1) * hd] = (acc[:, :hd] / acc[:, hd:hd + 1]).astype(o_ref.dtype)


def _fox_attn_prompt(q, qa, k, ka, v, bsz, seq, nh, hd, tq=512, hps=4):
    qblk = pl.BlockSpec((None, tq, hps * hd), lambda b, h, i: (b, i, h))
    kblk = pl.BlockSpec((None, seq, hps * hd), lambda b, h, i: (b, 0, h))
    return pl.pallas_call(
        functools.partial(_fox_attn_kernel, tq=tq, hd=hd),
        grid=(bsz, nh // hps, seq // tq),
        in_specs=[qblk, qblk, kblk, kblk, kblk],
        out_specs=qblk,
        out_shape=jax.ShapeDtypeStruct((bsz, seq, nh * hd), BF16),
        compiler_params=_cparams(("arbitrary", "arbitrary", "arbitrary")),
        name="fox_attn_prompt",
    )(q, qa, k, ka, v)


def _route(g_logits, e_logits):
    m = g_logits.shape[0]
    gi = lax.broadcasted_iota(jnp.int32, (m, N_GROUPS), 1)
    ei = lax.broadcasted_iota(jnp.int32, (m, N_EXPERTS), 1)
    gmax = jnp.max(g_logits, axis=1, keepdims=True)
    g_idx = jnp.min(jnp.where(g_logits == gmax, gi, N_GROUPS), axis=1, keepdims=True)
    g_gate = 1.0 / jnp.sum(jnp.exp(g_logits - gmax), axis=1, keepdims=True)
    in_group = (ei // EXP_PER_GROUP) == g_idx
    e1 = jnp.where(in_group, e_logits, -jnp.inf)
    top1 = jnp.max(e1, axis=1, keepdims=True)
    i1 = jnp.min(jnp.where(e1 == top1, ei, N_EXPERTS), axis=1, keepdims=True)
    e2 = jnp.where(ei == i1, -jnp.inf, e1)
    top2 = jnp.max(e2, axis=1, keepdims=True)
    i2 = jnp.min(jnp.where(e2 == top2, ei, N_EXPERTS), axis=1, keepdims=True)
    r = jnp.exp(top2 - top1)
    w1 = g_gate / (1.0 + r)
    w2 = g_gate * r / (1.0 + r)
    comb = jnp.where(ei == i1, w1, 0.0) + jnp.where(ei == i2, w2, 0.0)
    two = lax.broadcasted_iota(jnp.int32, (m, 2), 1)
    ids = jnp.where(two == 0, i1, i2)
    wts = jnp.where(two == 0, w1, w2)
    return comb, ids, wts


def _router_kernel(x_ref, g_ref, sh_ref, sc_ref, wg_ref, bg_ref, we_ref, be_ref, h_ref, comb_ref, ids_ref, wts_ref):
    y = _rmsnorm(x_ref[...], g_ref[...])
    h = y * (1.0 + sc_ref[...]) + sh_ref[...]
    h_ref[...] = h.astype(h_ref.dtype)
    g_logits = _dot_lp(h, wg_ref[...]) + bg_ref[...]
    e_logits = _dot_lp(h, we_ref[...]) + be_ref[...]
    comb, ids, wts = _route(g_logits, e_logits)
    comb_ref[...] = comb
    ids_ref[...] = ids
    wts_ref[...] = wts


def _router(x, g, sh_spec, sc_spec, mod_arrays, w_grp, b_grp, w_exp, b_exp, tm, h_dtype):
    t, d = x.shape
    full = lambda shape: pl.BlockSpec(shape, lambda i: (0,) * len(shape))
    row = lambda n: pl.BlockSpec((tm, n), lambda i: (i, 0))
    return pl.pallas_call(
        _router_kernel,
        grid=(t // tm,),
        in_specs=[row(d), full((1, d)), sh_spec, sc_spec,
                  full((d, N_GROUPS)), full((1, N_GROUPS)), full((d, N_EXPERTS)), full((1, N_EXPERTS))],
        out_specs=(row(d), row(N_EXPERTS), row(2), row(2)),
        out_shape=(jax.ShapeDtypeStruct((t, d), h_dtype), jax.ShapeDtypeStruct((t, N_EXPERTS), F32),
                   jax.ShapeDtypeStruct((t, 2), jnp.int32), jax.ShapeDtypeStruct((t, 2), F32)),
        compiler_params=_cparams(("arbitrary",)),
        name="moe_router",
    )(x, g.reshape(1, d), *mod_arrays, w_grp, b_grp.reshape(1, N_GROUPS), w_exp, b_exp.reshape(1, N_EXPERTS))


MOE_TB = 512
MOE_CH = BF16_ROWS
MOE_CPT = 40
MOE_NCH = (2 * MOE_TB + N_EXPERTS * (MOE_CH - 1)) // MOE_CH + 1
MOE_NR = MOE_NCH * MOE_CH
MOE_ROUTE_ROWS = 24


def _route_sort_kernel(x_ref, g_ref, sh_ref, sc_ref, wt_ref, bt_ref, srt_ref, pos_ref, nch_ref, yz_ref):
    yz_ref[...] = jnp.zeros(yz_ref.shape, yz_ref.dtype)
    tb, d = x_ref.shape
    nr = srt_ref.shape[0]
    h = _rmsnorm(x_ref[...], g_ref[...]) * (1.0 + sc_ref[...]) + sh_ref[...]
    hb = h.astype(BF16)
    lg = lax.dot_general(wt_ref[...].astype(BF16), hb, (((1,), (1,)), ((), ())), preferred_element_type=F32) + bt_ref[...]
    grow = lax.broadcasted_iota(jnp.int32, (8, tb), 0)
    gl = jnp.where(grow < N_GROUPS, lg[0:8], -jnp.inf)
    el = lg[8:8 + N_EXPERTS]
    erow = lax.broadcasted_iota(jnp.int32, (N_EXPERTS, tb), 0)
    gmax = jnp.max(gl, axis=0, keepdims=True)
    g_idx = jnp.min(jnp.where(gl == gmax, grow, 8), axis=0, keepdims=True)
    g_gate = 1.0 / jnp.sum(jnp.exp(gl - gmax), axis=0, keepdims=True)
    e1 = jnp.where(erow // EXP_PER_GROUP == g_idx, el, -jnp.inf)
    top1 = jnp.max(e1, axis=0, keepdims=True)
    i1 = jnp.min(jnp.where(e1 == top1, erow, N_EXPERTS), axis=0, keepdims=True)
    e2 = jnp.where(erow == i1, -jnp.inf, e1)
    top2 = jnp.max(e2, axis=0, keepdims=True)
    i2 = jnp.min(jnp.where(e2 == top2, erow, N_EXPERTS), axis=0, keepdims=True)
    r = jnp.exp(top2 - top1)
    w1 = g_gate / (1.0 + r)
    w2 = g_gate * r / (1.0 + r)

    oh = jnp.where(erow == i1, 1.0, jnp.where(erow == i2, 1.0, 0.0))
    nch = (jnp.sum(oh, axis=1, keepdims=True).astype(jnp.int32) + (MOE_CH - 1)) // MOE_CH
    lr = lax.broadcasted_iota(jnp.int32, (N_EXPERTS, N_EXPERTS), 0)
    lc = lax.broadcasted_iota(jnp.int32, (N_EXPERTS, N_EXPERTS), 1)
    before_e = jnp.where(lc < lr, 1.0, 0.0).astype(BF16)
    nch_b = jnp.broadcast_to(nch.astype(F32), (N_EXPERTS, LANES)).astype(BF16)
    off_rows = jnp.dot(before_e, nch_b, preferred_element_type=F32)[:, 0:1] * MOE_CH
    tr = lax.broadcasted_iota(jnp.int32, (tb, tb), 0)
    tc = lax.broadcasted_iota(jnp.int32, (tb, tb), 1)
    before_t = jnp.where(tr < tc, 1.0, 0.0).astype(BF16)
    rank = jnp.dot(oh.astype(BF16), before_t, preferred_element_type=F32)
    row_of = off_rows + rank
    p1 = jnp.sum(jnp.where(erow == i1, row_of, 0.0), axis=0, keepdims=True).astype(jnp.int32)
    p2 = jnp.sum(jnp.where(erow == i2, row_of, 0.0), axis=0, keepdims=True).astype(jnp.int32)

    rowi = lax.broadcasted_iota(jnp.int32, (nr, tb), 0)
    hit1 = rowi == p1
    hit2 = rowi == p2
    pm = jnp.where(hit1, 1.0, jnp.where(hit2, 1.0, 0.0)).astype(BF16)
    cw = 512
    for c0 in range(0, d, cw):
        srt_ref[:, c0:c0 + cw] = jnp.dot(pm, hb[:, c0:c0 + cw], preferred_element_type=F32).astype(BF16)
    wcol = jnp.sum(jnp.where(hit1, w1, jnp.where(hit2, w2, 0.0)), axis=1, keepdims=True)
    rnd = lambda v: v.astype(BF16).astype(F32)
    hi = rnd(wcol)
    mid = rnd(wcol - hi)
    lo = (wcol - hi) - mid
    lane = lax.broadcasted_iota(jnp.int32, (nr, LANES), 1)
    extra = jnp.where(lane == 0, hi, jnp.where(lane == 1, mid, jnp.where(lane == 2, lo, 0.0)))
    srt_ref[:, d:] = extra.astype(BF16)
    prow = lax.broadcasted_iota(jnp.int32, (8, tb), 0)
    pos_ref[...] = jnp.where(prow == 0, p1, jnp.where(prow == 1, p2, 0))
    nch_ref[...] = jnp.broadcast_to(nch, (N_EXPERTS, LANES))


def _route_sort_prompt(x, g, m4, layer, w_grp, b_grp, w_exp, b_exp, seq):
    t, d = x.shape
    nblk = t // MOE_TB
    tpb = seq // MOE_TB
    pad = jnp.zeros((8 - N_GROUPS, d), F32)
    wt = jnp.concatenate([w_grp.T, pad, w_exp.T], axis=0)
    bt = jnp.concatenate([b_grp, jnp.zeros((8 - N_GROUPS,), F32), b_exp]).reshape(MOE_ROUTE_ROWS, 1)
    full = lambda shape: pl.BlockSpec(shape, lambda i: (0,) * len(shape))
    n_steps = _moe_result_shape(nblk, d)[0] // MOE_NR
    assert n_steps * MOE_NR == _moe_result_shape(nblk, d)[0]
    blk = lambda i: jnp.minimum(i, nblk - 1)
    mod = lambda slot: pl.BlockSpec((None, None, 1, d), lambda i: (layer, blk(i) // tpb, 0, slot))
    return pl.pallas_call(
        _route_sort_kernel,
        grid=(n_steps,),
        in_specs=[pl.BlockSpec((MOE_TB, d), lambda i: (blk(i), 0)), full((1, d)), mod(3), mod(4),
                  full((MOE_ROUTE_ROWS, d)), full((MOE_ROUTE_ROWS, 1))],
        out_specs=(pl.BlockSpec((MOE_NR, d + LANES), lambda i: (blk(i), 0)),
                   pl.BlockSpec((None, 8, MOE_TB), lambda i: (blk(i), 0, 0)),
                   pl.BlockSpec((None, N_EXPERTS, LANES), lambda i: (blk(i), 0, 0)),
                   pl.BlockSpec((MOE_NR, d), lambda i: (i, 0))),
        out_shape=(jax.ShapeDtypeStruct((nblk * MOE_NR, d + LANES), BF16),
                   jax.ShapeDtypeStruct((nblk, 8, MOE_TB), jnp.int32),
                   jax.ShapeDtypeStruct((nblk, N_EXPERTS, LANES), jnp.int32),
                   jax.ShapeDtypeStruct(_moe_result_shape(nblk, d), BF16)),
        compiler_params=_cparams(("arbitrary",)),
        name="moe_route_sort",
    )(x, g.reshape(1, d), m4, m4, wt, bt)


def _moe_result_shape(nblk, d):
    return (nblk * MOE_NR + 2 * MOE_CPT * MOE_CH, d)


def _moe_n_tiles(nblk):
    return (nblk * (MOE_NCH - 1)) // MOE_CPT + N_EXPERTS


def _moe_schedule(nch):
    nblk = nch.shape[0]
    n_tiles = _moe_n_tiles(nblk)
    off_c = jnp.cumsum(nch, axis=1) - nch
    n_e = nch.sum(axis=0)
    tiles_e = (n_e + MOE_CPT - 1) // MOE_CPT
    tile_end = jnp.cumsum(tiles_e)
    tile_start = tile_end - tiles_e
    tau = jnp.arange(n_tiles, dtype=jnp.int32)
    e = jnp.minimum((tau[:, None] >= tile_end[None, :]).sum(axis=1), N_EXPERTS - 1).astype(jnp.int32)
    oe = (e[:, None] == jnp.arange(N_EXPERTS, dtype=jnp.int32)[None, :]).astype(jnp.int32)
    pick_e = lambda table: (oe * table[None, :]).sum(axis=1)
    rows_e = lambda table: (oe[:, :, None] * table.T[None, :, :]).sum(axis=1)
    g = (tau - pick_e(tile_start))[:, None] * MOE_CPT + jnp.arange(MOE_CPT, dtype=jnp.int32)[None, :]
    valid = (tau < tile_end[-1])[:, None] & (g < pick_e(n_e)[:, None])
    nch_e = rows_e(nch)
    cum_b = jnp.cumsum(nch_e, axis=1)
    b = jnp.minimum((g[:, :, None] >= cum_b[:, None, :]).sum(axis=-1), nblk - 1)
    ob = (b[:, :, None] == jnp.arange(nblk, dtype=jnp.int32)[None, None, :]).astype(jnp.int32)
    pick_b = lambda table: (ob * table[:, None, :]).sum(axis=-1)
    chunk = b * MOE_NCH + pick_b(rows_e(off_c)) + (g - pick_b(cum_b - nch_e))
    src = jnp.where(valid, chunk, MOE_NCH - 1)
    spare = nblk * MOE_NCH + (tau % 2)[:, None] * MOE_CPT + jnp.arange(MOE_CPT, dtype=jnp.int32)[None, :]
    dst = jnp.where(valid, chunk, spare)
    return (e, src.reshape(-1).astype(jnp.int32), dst.reshape(-1).astype(jnp.int32),
            tile_end[-1:].astype(jnp.int32))


def _moe_expert_kernel(te_ref, src_ref, dst_ref, nv_ref, srt_hbm, w1_ref, w3_ref, w2_ref, yin_hbm, y_hbm,
                       xbuf, ybuf, w1b, w3b, w2b, sem_in, sem_out, *, n_tiles, d):
    del yin_hbm
    t = pl.program_id(0)
    slot = t % 2

    def in_copy(tile, s, c):
        row = pl.multiple_of(src_ref[tile * MOE_CPT + c] * MOE_CH, MOE_CH)
        return pltpu.make_async_copy(srt_hbm.at[pl.ds(row, MOE_CH), :], xbuf.at[s, pl.ds(c * MOE_CH, MOE_CH), :],
                                     sem_in.at[s])

    def out_copy(tile, s, c):
        row = pl.multiple_of(dst_ref[tile * MOE_CPT + c] * MOE_CH, MOE_CH)
        return pltpu.make_async_copy(ybuf.at[s, pl.ds(c * MOE_CH, MOE_CH), :], y_hbm.at[pl.ds(row, MOE_CH), :],
                                     sem_out.at[s])

    @pl.when(t == 0)
    def _():
        for c in range(MOE_CPT):
            in_copy(0, 0, c).start()

    @pl.when(t + 1 < n_tiles)
    def _():
        for c in range(MOE_CPT):
            in_copy(t + 1, 1 - slot, c).start()

    for c in range(MOE_CPT):
        in_copy(t, slot, c).wait()

    @pl.when(t >= 2)
    def _():
        for c in range(MOE_CPT):
            out_copy(t - 2, slot, c).wait()

    @pl.when((t == 0) | (te_ref[t] != te_ref[jnp.maximum(t - 1, 0)]))
    def _():
        w1b[...] = w1_ref[...].astype(BF16)
        w3b[...] = w3_ref[...].astype(BF16)
        w2b[...] = w2_ref[...].astype(BF16)

    @pl.when(t < nv_ref[0])
    def _():
        x = xbuf[slot]
        h = x[:, :d]
        wcol = jnp.sum(x[:, d:].astype(F32), axis=1, keepdims=True)
        a = jnp.dot(h, w1b[...], preferred_element_type=F32)
        b = jnp.dot(h, w3b[...], preferred_element_type=F32)
        hid = (_silu(a) * b * wcol).astype(BF16)
        ybuf[slot] = jnp.dot(hid, w2b[...], preferred_element_type=F32).astype(BF16)

    for c in range(MOE_CPT):
        out_copy(t, slot, c).start()

    @pl.when(t == n_tiles - 1)
    def _():
        for c in range(MOE_CPT):
            out_copy(t, slot, c).wait()
        if n_tiles >= 2:
            for c in range(MOE_CPT):
                out_copy(t - 1, 1 - slot, c).wait()


def _moe_experts(srt, y_zero, te, src, dst, n_used, w1, w3, w2, layer, nblk):
    d = srt.shape[1] - LANES
    f = w1.shape[-1]
    n_tiles = _moe_n_tiles(nblk)
    tile_rows = MOE_CPT * MOE_CH
    y_rows = _moe_result_shape(nblk, d)[0]
    assert y_zero.shape == (y_rows, d)
    wspec = lambda shape: pl.BlockSpec((None, None) + shape, lambda t, te, src, dst, nv: (layer, te[t], 0, 0))
    return pl.pallas_call(
        functools.partial(_moe_expert_kernel, n_tiles=n_tiles, d=d),
        grid_spec=pltpu.PrefetchScalarGridSpec(
            num_scalar_prefetch=4,
            grid=(n_tiles,),
            in_specs=[pl.BlockSpec(memory_space=pl.ANY), wspec((d, f)), wspec((d, f)), wspec((f, d)),
                      pl.BlockSpec(memory_space=pl.ANY)],
            out_specs=pl.BlockSpec(memory_space=pl.ANY),
            scratch_shapes=[pltpu.VMEM((2, tile_rows, d + LANES), BF16), pltpu.VMEM((2, tile_rows, d), BF16),
                            pltpu.VMEM((d, f), BF16), pltpu.VMEM((d, f), BF16), pltpu.VMEM((f, d), BF16),
                            pltpu.SemaphoreType.DMA((2,)), pltpu.SemaphoreType.DMA((2,))]),
        out_shape=jax.ShapeDtypeStruct((y_rows, d), BF16),
        input_output_aliases={8: 0},
        compiler_params=_cparams(("arbitrary",)),
        name="moe_experts",
    )(te, src, dst, n_used, srt, w1, w3, w2, y_zero)


def _unsort_kernel(*refs, last):
    if last:
        y_ref, pos_ref, x_ref, gate_ref, g_ref, out_ref, xn_ref = refs
    else:
        y_ref, pos_ref, x_ref, gate_ref, g_ref, sh_ref, sc_ref, xn_ref, h_ref = refs
    nr = y_ref.shape[0]
    tb, d = x_ref.shape
    rowi = lax.broadcasted_iota(jnp.int32, (nr, tb), 0)
    pm = jnp.where(rowi == pos_ref[0:1, :], 1.0, jnp.where(rowi == pos_ref[1:2, :], 1.0, 0.0)).astype(BF16)
    cw = 512
    for c0 in range(0, d, cw):
        y = lax.dot_general(pm, y_ref[:, c0:c0 + cw], (((0,), (0,)), ((), ())), preferred_element_type=F32)
        xn_ref[:, c0:c0 + cw] = x_ref[:, c0:c0 + cw] + gate_ref[:, c0:c0 + cw] * y
    normed = _rmsnorm(xn_ref[...], g_ref[...])
    if last:
        out_ref[...] = normed
    else:
        h_ref[...] = (normed * (1.0 + sc_ref[...]) + sh_ref[...]).astype(h_ref.dtype)


def _moe_unsort(y_srt, pos, x, m4, layer, seq, g_next, last):
    t, d = x.shape
    tpb = seq // MOE_TB
    rows = pl.BlockSpec((MOE_TB, d), lambda i: (i, 0))
    ins = [y_srt, pos, x, m4, g_next.reshape(1, d)]
    in_specs = [pl.BlockSpec((MOE_NR, d), lambda i: (i, 0)), pl.BlockSpec((None, 8, MOE_TB), lambda i: (i, 0, 0)),
                rows, _mod_spec(layer, 5, tpb, d), pl.BlockSpec((1, d), lambda i: (0, 0))]
    if last:
        out_shape, out_specs = jax.ShapeDtypeStruct((t, d), F32), rows
        scratch = [pltpu.VMEM((MOE_TB, d), F32)]
    else:
        ins += [m4, m4]
        in_specs += [_mod_spec(layer + 1, 0, tpb, d), _mod_spec(layer + 1, 1, tpb, d)]
        out_shape = (jax.ShapeDtypeStruct((t, d), F32), jax.ShapeDtypeStruct((t, d), BF16))
        out_specs, scratch = (rows, rows), []
    return pl.pallas_call(
        functools.partial(_unsort_kernel, last=last),
        grid=(t // MOE_TB,),
        in_specs=in_specs, out_specs=out_specs, out_shape=out_shape, scratch_shapes=scratch,
        compiler_params=_cparams(("arbitrary",)),
        name="moe_unsort",
    )(*ins)


def _s_in_kernel(*refs, act, has_bias):
    if has_bias:
        x_ref, g_ref, sh_ref, sc_ref, w_ref, b_ref, o_ref = refs
    else:
        x_ref, g_ref, sh_ref, sc_ref, w_ref, o_ref = refs
    h = _rmsnorm(x_ref[...], g_ref[...]) * (1.0 + sc_ref[...]) + sh_ref[...]
    y = _dot_lp(h, w_ref[...])
    if has_bias:
        y = y + b_ref[...]
    if act == "gelu":
        y = _gelu_tanh(y)
    elif act == "logsig":
        y = _log_sigmoid(y)
    o_ref[...] = y


def _s_in(x, g, sh, sc, w, bias, act, tn):
    m, d = x.shape
    n = w.shape[1]
    tn = min(tn, n)
    full = pl.BlockSpec((m, d), lambda j: (0, 0))
    ins = [x, g.reshape(1, d), sh, sc, w]
    in_specs = [full, pl.BlockSpec((1, d), lambda j: (0, 0)), full, full, pl.BlockSpec((d, tn), lambda j: (0, j))]
    if bias is not None:
        ins.append(bias.reshape(1, n))
        in_specs.append(pl.BlockSpec((1, tn), lambda j: (0, j)))
    return pl.pallas_call(
        functools.partial(_s_in_kernel, act=act, has_bias=bias is not None),
        grid=(n // tn,),
        in_specs=in_specs,
        out_specs=pl.BlockSpec((m, tn), lambda j: (0, j)),
        out_shape=jax.ShapeDtypeStruct((m, n), F32),
        compiler_params=_cparams(("arbitrary",)),
        name="s_in_" + act,
    )(*ins)


def _s_out_kernel(*refs, has_bias):
    if has_bias:
        a_ref, w_ref, b_ref, x_ref, gate_ref, o_ref = refs
    else:
        a_ref, w_ref, x_ref, gate_ref, o_ref = refs
    y = _dot_lp(a_ref[...], w_ref[...])
    if has_bias:
        y = y + b_ref[...]
    o_ref[...] = x_ref[...] + gate_ref[...] * y


def _s_out(a, w, bias, x, gate, tn=512):
    m, k = a.shape
    n = w.shape[1]
    col = pl.BlockSpec((m, tn), lambda j: (0, j))
    ins = [a, w]
    in_specs = [pl.BlockSpec((m, k), lambda j: (0, 0)), pl.BlockSpec((k, tn), lambda j: (0, j))]
    if bias is not None:
        ins.append(bias.reshape(1, n))
        in_specs.append(pl.BlockSpec((1, tn), lambda j: (0, j)))
    ins += [x, gate]
    in_specs += [col, col]
    return pl.pallas_call(
        functools.partial(_s_out_kernel, has_bias=bias is not None),
        grid=(n // tn,),
        in_specs=in_specs, out_specs=col,
        out_shape=jax.ShapeDtypeStruct((m, n), F32),
        compiler_params=_cparams(("arbitrary",)),
        name="s_out",
    )(*ins)


def _s_sgu_kernel(z_ref, lng_ref, lnb_ref, w00_ref, b0_ref, gv_ref, o_ref, *, width):
    u = z_ref[:, :width]
    v = z_ref[:, width:]
    mu = jnp.mean(v, axis=-1, keepdims=True)
    vc = v - mu
    vn = vc * lax.rsqrt(jnp.mean(vc * vc, axis=-1, keepdims=True) + LN_EPS) * lng_ref[...] + lnb_ref[...]
    gv_ref[...] = vn
    o_ref[...] = u * (vn * w00_ref[...] + b0_ref[...])


def _s_sgu(z, ln_g, ln_b, ws, bs):
    m, n2 = z.shape
    width = n2 // 2
    gd = width // GM_GROUPS
    w00 = jnp.repeat(ws[:, 0, 0], gd).reshape(1, width)
    b0 = jnp.repeat(bs[:, 0], gd).reshape(1, width)
    vec = pl.BlockSpec((1, width), lambda: (0, 0))
    blk = pl.BlockSpec((m, width), lambda: (0, 0))
    return pl.pallas_call(
        functools.partial(_s_sgu_kernel, width=width),
        in_specs=[pl.BlockSpec((m, n2), lambda: (0, 0)), vec, vec, vec, vec],
        out_specs=(blk, blk),
        out_shape=(jax.ShapeDtypeStruct((m, width), F32), jax.ShapeDtypeStruct((m, width), F32)),
        name="s_sgu",
    )(z, ln_g.reshape(1, width), ln_b.reshape(1, width), w00, b0)


def _s_attn_kernel(pt_ref, q_ref, kn_ref, vn_ref, lfn_ref, cnf_ref, *refs, n_steps, pps, page, nh, scale):
    del pt_ref
    ck_refs, cv_refs, clf_refs = refs[:pps], refs[pps:2 * pps], refs[2 * pps:3 * pps]
    o_ref, m_ref, l_ref, acc_ref, carry_ref = refs[3 * pps:]
    p = pl.program_id(1)
    pflat = page * nh
    flat = pps * pflat
    lanes = m_ref.shape[1]

    @pl.when(p == 0)
    def _():
        m_ref[...] = jnp.full_like(m_ref, NEG_INF)
        l_ref[...] = jnp.zeros_like(l_ref)
        acc_ref[...] = jnp.zeros_like(acc_ref)
        carry_ref[...] = jnp.zeros_like(carry_ref)

    def per_head(x, op):
        y = x[:, :lanes]
        for i in range(1, flat // lanes):
            y = op(y, x[:, i * lanes:(i + 1) * lanes])
        s = nh
        while s < lanes:
            y = op(y, pltpu.roll(y, s, axis=1))
            s *= 2
        return y

    tile = lambda y: jnp.concatenate([y] * (flat // lanes), axis=1)

    def to_col(row):
        r = lax.broadcasted_iota(jnp.int32, (nh, nh), 0)
        c = lax.broadcasted_iota(jnp.int32, (nh, nh), 1)
        return jnp.sum(jnp.where(r == c, jnp.broadcast_to(row, (nh, nh)), 0.0), axis=1, keepdims=True)

    qb = q_ref[...].astype(BF16)
    own_head = (lax.broadcasted_iota(jnp.int32, (nh, pflat), 1) % nh) == lax.broadcasted_iota(jnp.int32, (nh, pflat), 0)

    lf = jnp.concatenate([r[...] for r in clf_refs], axis=1)
    lane_idx = lax.broadcasted_iota(jnp.int32, (1, flat), 1)
    inc = lf
    step = nh
    while step < flat:
        if step % lanes == 0:
            shifted = jnp.concatenate([inc[:, step:], jnp.zeros((1, step), F32)], axis=1)
        else:
            shifted = jnp.where(lane_idx < flat - step, pltpu.roll(inc, flat - step, axis=1), 0.0)
        inc = inc + shifted
        step *= 2
    rest = (inc - lf) + tile(carry_ref[...])
    carry_ref[...] = carry_ref[...] + per_head(lf, jnp.add)

    def own_logits(k_ref):
        sf = lax.dot_general(qb, k_ref[...].astype(BF16), (((1,), (1,)), ((), ())), preferred_element_type=F32)
        return jnp.sum(jnp.where(own_head, sf, 0.0), axis=0, keepdims=True)

    qk = jnp.concatenate([own_logits(r) for r in ck_refs], axis=1)
    s = qk * scale + jnp.concatenate([cnf_ref[...]] * pps, axis=1) + rest
    m_old = m_ref[...]
    m_new = jnp.maximum(m_old, per_head(s, jnp.maximum))
    alpha = jnp.exp(m_old - m_new)
    pr = jnp.exp(s - tile(m_new))
    l_ref[...] = l_ref[...] * alpha + per_head(pr, jnp.add)
    m_ref[...] = m_new
    pv = None
    for j, v_ref in enumerate(cv_refs):
        pr_j = pr[:, j * pflat:(j + 1) * pflat]
        pm = jnp.where(own_head, jnp.broadcast_to(pr_j, (nh, pflat)), 0.0).astype(BF16)
        d = jnp.dot(pm, v_ref[...].astype(BF16), preferred_element_type=F32)
        pv = d if pv is None else pv + d
    acc_ref[...] = acc_ref[...] * to_col(alpha[:, :nh]) + pv

    @pl.when(p == n_steps - 1)
    def _():
        rnd = lambda x: x.astype(BF16).astype(F32)
        c_new = lfn_ref[...]
        sn = lax.dot_general(qb, kn_ref[...].astype(BF16), (((1,), (1,)), ((), ())), preferred_element_type=F32)
        r = lax.broadcasted_iota(jnp.int32, (nh, nh), 0)
        c = lax.broadcasted_iota(jnp.int32, (nh, nh), 1)
        s_n = jnp.sum(jnp.where(r == c, sn, 0.0), axis=0, keepdims=True) * scale + (c_new - c_new)
        m_o = m_ref[:, :nh]
        m_n = jnp.maximum(m_o, s_n)
        alpha_n = jnp.exp(m_o - m_n)
        pr_n = jnp.exp(s_n - m_n)
        l_n = l_ref[:, :nh] * alpha_n + pr_n
        acc = acc_ref[...] * to_col(alpha_n) + to_col(rnd(pr_n)) * rnd(vn_ref[...])
        o_ref[...] = acc / to_col(l_n)


def _s_attn(q, k_new, v_new, lf_new, cache_k, cache_v, cache_logf, page_table, slot):
    bsz, d = q.shape
    n_fox, n_pool, page, nh, hd = cache_k.shape
    n_pages = page_table.shape[1]
    flat = page * nh
    pps = next(c for c in (4, 2, 1) if n_pages % c == 0)
    n_steps = n_pages // pps
    heads = pl.BlockSpec((None, nh, hd), lambda b, p, pt: (b, 0, 0))
    per_seq = lambda n: pl.BlockSpec((None, 1, n), lambda b, p, pt: (b, 0, 0))

    def pg_idx(j):
        return lambda b, p, pt: (slot, pt[b * n_pages + (n_steps - 1 - p) * pps + j], 0, 0)

    cache_k = cache_k.reshape(n_fox, n_pool, flat, hd)
    cache_v = cache_v.reshape(n_fox, n_pool, flat, hd)
    clf_flat = cache_logf.reshape(n_fox, n_pool, 1, flat)
    cn_flat = jnp.tile(lf_new, (1, page)).reshape(bsz, 1, flat)
    kv_specs = [pl.BlockSpec((None, None, flat, hd), pg_idx(j)) for j in range(pps)]
    lf_specs = [pl.BlockSpec((None, None, 1, flat), pg_idx(j)) for j in range(pps)]
    out = pl.pallas_call(
        functools.partial(_s_attn_kernel, n_steps=n_steps, pps=pps, page=page, nh=nh, scale=hd ** -0.5),
        grid_spec=pltpu.PrefetchScalarGridSpec(
            num_scalar_prefetch=1,
            grid=(bsz, n_steps),
            in_specs=[heads, heads, heads, per_seq(nh), per_seq(flat)] + kv_specs + kv_specs + lf_specs,
            out_specs=heads,
            scratch_shapes=[pltpu.VMEM((1, LANES), F32), pltpu.VMEM((1, LANES), F32), pltpu.VMEM((nh, hd), F32),
                            pltpu.VMEM((1, LANES), F32)]),
        out_shape=jax.ShapeDtypeStruct((bsz, nh, hd), F32),
        compiler_params=_cparams(("arbitrary", "arbitrary")),
        name="s_attn",
    )(page_table.reshape(-1), q.reshape(bsz, nh, hd), k_new.reshape(bsz, nh, hd), v_new.reshape(bsz, nh, hd),
      lf_new.reshape(bsz, 1, nh), cn_flat, *([cache_k] * pps), *([cache_v] * pps), *([clf_flat] * pps))
    return out.reshape(bsz, d)


def _s_moe_kernel(eid_ref, h_ref, comb_ref, w1_ref, w3_ref, w2_ref, x_ref, gate_ref, o_ref, acc_ref, *, n_pairs):
    p = pl.program_id(0)
    e = eid_ref[p]

    @pl.when(p == 0)
    def _():
        acc_ref[...] = jnp.zeros_like(acc_ref)

    @pl.when((p == 0) | (e != eid_ref[jnp.maximum(p - 1, 0)]))
    def _():
        h = h_ref[...]
        a = _dot_lp(h, w1_ref[...])
        b = _dot_lp(h, w3_ref[...])
        comb = comb_ref[...]
        wcol = jnp.sum(jnp.where(lax.broadcasted_iota(jnp.int32, comb.shape, 1) == e, comb, 0.0), axis=1, keepdims=True)
        acc_ref[...] += _dot_lp(_silu(a) * b * wcol, w2_ref[...])

    @pl.when(p == n_pairs - 1)
    def _():
        o_ref[...] = x_ref[...] + gate_ref[...] * acc_ref[...]


def _s_moe(h, ids, comb, w1, w3, w2, x, gate, layer):
    m, d = h.shape
    f = w1.shape[-1]
    n_pairs = 2 * m
    eid = ids.reshape(-1)
    idx = jnp.arange(n_pairs, dtype=jnp.int32)
    rank = ((eid[None, :] < eid[:, None]) | ((eid[None, :] == eid[:, None]) & (idx[None, :] < idx[:, None]))).sum(axis=1)
    eid_sorted = ((rank[None, :] == idx[:, None]) * eid[None, :]).sum(axis=1).astype(jnp.int32)
    full = lambda shape: pl.BlockSpec(shape, lambda p, eid: (0,) * len(shape))
    wspec = lambda shape: pl.BlockSpec((None, None) + shape, lambda p, eid: (layer, eid[p], 0, 0))
    return pl.pallas_call(
        functools.partial(_s_moe_kernel, n_pairs=n_pairs),
        grid_spec=pltpu.PrefetchScalarGridSpec(
            num_scalar_prefetch=1,
            grid=(n_pairs,),
            in_specs=[full((m, d)), full((m, N_EXPERTS)), wspec((d, f)), wspec((d, f)), wspec((f, d)),
                      full((m, d)), full((m, d))],
            out_specs=full((m, d)),
            scratch_shapes=[pltpu.VMEM((m, d), F32)]),
        out_shape=jax.ShapeDtypeStruct((m, d), F32),
        compiler_params=_cparams(("arbitrary",)),
        name="s_moe",
    )(eid_sorted, h, comb, w1, w3, w2, x, gate)


def kernel(x_prompt, x_sample, cache_k, cache_v, cache_logf, page_table, c_prompt, c_sample,
           w_ada, b_ada, norm1_g, norm2_g, final_g,
           gm_w_in, gm_b_in, gm_ln_g, gm_ln_b, gm_ws, gm_bs, gm_w_out, gm_b_out,
           fox_w_qkv, fox_w_f, fox_b_f, fox_w_o,
           moe_w_grp, moe_b_grp, moe_w_exp, moe_b_exp, moe_w1, moe_w3, moe_w2):
    bsz, seq, d = x_prompt.shape
    dbsz = x_sample.shape[0]
    depth = w_ada.shape[0]
    nh, hd = cache_k.shape[3], cache_k.shape[4]
    t = bsz * seq
    assert bsz + dbsz <= ADA_ROWS and x_sample.shape[1] == 1

    c16 = jnp.concatenate([c_prompt, c_sample, jnp.zeros((ADA_ROWS - bsz - dbsz, d), F32)], axis=0)
    m_all = _ada_all(c16, w_ada, b_ada)
    m4 = m_all.reshape(depth, ADA_ROWS, 1, 6 * d)

    xp = x_prompt.reshape(t, d)
    xs = x_sample.reshape(dbsz, d)
    k_p, v_p, lf_p, k_s, v_s, lf_s, gv_s = [], [], [], [], [], [], []
    hp = _norm_mod_prompt(xp, norm1_g[0], m4, 0, 0, seq)
    y_prompt = None

    for i in range(depth):
        slot = i // 2
        ms = m_all[i, bsz:bsz + dbsz].reshape(dbsz, 6, d)
        if i % 2 == 0:
            z = _mm_prompt("gelu", hp, gm_w_in[slot], bias=gm_b_in[slot])
            gated = _sgu_prompt(z, gm_ln_g[slot], gm_ln_b[slot], gm_ws[slot], gm_bs[slot])
            xp = _mm_prompt("resid", gated, gm_w_out[slot], bias=gm_b_out[slot], x=xp, m4=m4, layer=i, gate_slot=2,
                            seq=seq)
            zs = _s_in(xs, norm1_g[i], ms[:, 0], ms[:, 1], gm_w_in[slot], gm_b_in[slot], "gelu", 512)
            gv, gs = _s_sgu(zs, gm_ln_g[slot], gm_ln_b[slot], gm_ws[slot], gm_bs[slot])
            gv_s.append(gv)
            xs = _s_out(gs, gm_w_out[slot], gm_b_out[slot], xs, ms[:, 2])
        else:
            wqkv = fox_w_qkv[slot]
            q = _mm_prompt("proj_bf16", hp, wqkv, col0=0, n=d, out_scale=hd ** -0.5)
            kf, kb = _mm_prompt("proj_both", hp, wqkv, col0=d, n=d)
            vf, vb = _mm_prompt("proj_both", hp, wqkv, col0=2 * d, n=d)
            lf, qa, ka = _logf_prompt(hp, fox_w_f[slot], fox_b_f[slot], bsz, seq, hd)
            b3 = lambda a: a.reshape(bsz, seq, d)
            o = _fox_attn_prompt(b3(q), b3(qa), b3(kb), b3(ka), b3(vb), bsz, seq, nh, hd)
            xp = _mm_prompt("resid", o.reshape(t, d), fox_w_o[slot], bias=jnp.zeros((d,), F32), x=xp, m4=m4, layer=i,
                            gate_slot=2, seq=seq)
            k_p.append(kf.reshape(bsz, seq, nh, hd))
            v_p.append(vf.reshape(bsz, seq, nh, hd))
            lf_p.append(lf.reshape(bsz, seq, nh))
            qkv = _s_in(xs, norm1_g[i], ms[:, 0], ms[:, 1], wqkv, None, "none", 512)
            lfn = _s_in(xs, norm1_g[i], ms[:, 0], ms[:, 1], fox_w_f[slot], fox_b_f[slot], "logsig", 512)
            qs, kn, vn = qkv[:, :d], qkv[:, d:2 * d], qkv[:, 2 * d:]
            o_s = _s_attn(qs, kn, vn, lfn, cache_k, cache_v, cache_logf, page_table, slot)
            xs = _s_out(o_s, fox_w_o[slot], None, xs, ms[:, 2])
            k_s.append(kn.reshape(dbsz, 1, nh, hd))
            v_s.append(vn.reshape(dbsz, 1, nh, hd))
            lf_s.append(lfn.reshape(dbsz, 1, nh))

        srt, pos, nch, y_zero = _route_sort_prompt(xp, norm2_g[i], m4, i, moe_w_grp[i], moe_b_grp[i], moe_w_exp[i],
                                                   moe_b_exp[i], seq)
        te, src, dst, n_used = _moe_schedule(nch[:, :, 0])
        y_srt = _moe_experts(srt, y_zero, te, src, dst, n_used, moe_w1, moe_w3, moe_w2, i, t // MOE_TB)
        if i + 1 < depth:
            xp, hp = _moe_unsort(y_srt, pos, xp, m4, i, seq, norm1_g[i + 1], last=False)
        else:
            y_prompt = _moe_unsort(y_srt, pos, xp, m4, i, seq, final_g, last=True).reshape(bsz, seq, d)
        full8 = pl.BlockSpec((dbsz, d), lambda j: (0, 0))
        h2s, comb_s, ids, _ = _router(xs, norm2_g[i], full8, full8, (ms[:, 3], ms[:, 4]), moe_w_grp[i], moe_b_grp[i],
                                      moe_w_exp[i], moe_b_exp[i], dbsz, F32)
        xs = _s_moe(h2s, ids, comb_s, moe_w1, moe_w3, moe_w2, xs, ms[:, 5], i)

    y_sample = _final_norm(xs, final_g, dbsz).reshape(dbsz, 1, d)
    return (y_prompt, y_sample, jnp.stack(k_p), jnp.stack(v_p), jnp.stack(lf_p),
            jnp.stack(k_s), jnp.stack(v_s), jnp.stack(lf_s), jnp.stack(gv_s).reshape(len(gv_s), dbsz, 1, d))
```
